```python
import math
import jax, jax.numpy as jnp
from jax import lax
import numpy as np

D_MODEL = 2048
BATCH = 1
SEQ = 8192
DEPTH = 4

GRID_W = 64
CTX_LEN = 256
HEAD_DIM = 64
A_Q_HEADS = 16
A_KV_HEADS = 2
A_GROUP = A_Q_HEADS // A_KV_HEADS
B_Q_HEADS = 16
B_KV_HEADS = 2
B_GROUP = B_Q_HEADS // B_KV_HEADS
WINDOW = 128
BLOCK = 128
AXIS_DIM = HEAD_DIM // 2
ROPE_THETA = 10000.0
ATTN_SCALE = HEAD_DIM ** -0.5
A_Q = A_Q_HEADS * HEAD_DIM
A_KV = A_KV_HEADS * HEAD_DIM
B_Q = B_Q_HEADS * HEAD_DIM
B_KV = B_KV_HEADS * HEAD_DIM
MIX_WIDTH = A_Q + B_Q
IN_COLS = A_Q + 2 * A_KV + B_Q + 2 * B_KV
SPLIT_POINTS = (A_Q, A_Q + A_KV, A_Q + 2 * A_KV, A_Q + 2 * A_KV + B_Q, A_Q + 2 * A_KV + B_Q + B_KV)
N_EXPERTS = 32
TOP_K = 4
D_EXPERT = 512
SWIGLU_LIMIT = 7.0
SWIGLU_ALPHA = 1.702
DEEPNORM_ALPHA = (2.0 * DEPTH) ** 0.25
DEEPNORM_BETA = (8.0 * DEPTH) ** -0.25
LN_EPS = 1e-5
RMS_EPS = 1e-6

kernel_name = 'hymba_style_window_sink_axial_gqa_moe_prefix_ctx'


def _layer_norm(x, g=None, b=None):
    xf = x.astype(jnp.float32)
    mu = jnp.mean(xf, -1, keepdims=True)
    var = jnp.mean(jnp.square(xf - mu), -1, keepdims=True)
    y = (xf - mu) * lax.rsqrt(var + LN_EPS)
    if g is not None:
        y = y * g.astype(jnp.float32) + b.astype(jnp.float32)
    return y.astype(x.dtype)


def _rms_norm(x, g):
    xf = x.astype(jnp.float32)
    y = xf * lax.rsqrt(jnp.mean(jnp.square(xf), -1, keepdims=True) + RMS_EPS)
    return (y * g.astype(jnp.float32)).astype(x.dtype)


def _axial_rope_tables(rows, dtype):
    row = jnp.repeat(jnp.arange(rows, dtype=jnp.float32), GRID_W)
    col = jnp.tile(jnp.arange(GRID_W, dtype=jnp.float32), rows)
    inv = ROPE_THETA ** (-jnp.arange(0, AXIS_DIM, 2, dtype=jnp.float32) / AXIS_DIM)
    ar = row[:, None] * inv[None, :]
    ac = col[:, None] * inv[None, :]
    ang = jnp.concatenate([ar, ar, ac, ac], -1)
    return jnp.cos(ang).astype(dtype), jnp.sin(ang).astype(dtype)


def _rope(x, cos, sin):
    n = x.shape[1]
    shape = (1, n) + (1,) * (x.ndim - 3) + (HEAD_DIM,)
    x1, x2, x3, x4 = jnp.split(x, 4, axis=-1)
    rot = jnp.concatenate([-x2, x1, -x4, x3], -1)
    return x * cos.reshape(shape) + rot * sin.reshape(shape)


def _modulation(cond, w_ada, b_ada):
    m = jnp.einsum('...d,de->...e', jax.nn.silu(cond), w_ada) + b_ada
    return jnp.split(m, 6, axis=-1)


def _modulate(x, shift, scale):
    return _layer_norm(x) * (1.0 + scale) + shift


def _project(h, w_in, q_gain, k_gain, cos, sin):
    bsz, n, _ = h.shape
    p = jnp.einsum('bnd,de->bne', h, w_in)
    qa, ka, va, qb, kb, vb = jnp.split(p, SPLIT_POINTS, axis=-1)
    qa = qa.reshape(bsz, n, A_KV_HEADS, A_GROUP, HEAD_DIM)
    ka = ka.reshape(bsz, n, A_KV_HEADS, HEAD_DIM)
    va = va.reshape(bsz, n, A_KV_HEADS, HEAD_DIM)
    qb = _rms_norm(qb.reshape(bsz, n, B_KV_HEADS, B_GROUP, HEAD_DIM), q_gain)
    kb = _rms_norm(kb.reshape(bsz, n, B_KV_HEADS, HEAD_DIM), k_gain)
    vb = vb.reshape(bsz, n, B_KV_HEADS, HEAD_DIM)
    if cos is not None:
        qa, ka = _rope(qa, cos, sin), _rope(ka, cos, sin)
        qb, kb = _rope(qb, cos, sin), _rope(kb, cos, sin)
    return qa, ka, va, qb, kb, vb


def _sink_softmax(s, sink):
    sb = sink[:, :, None, None]
    m = jnp.maximum(jnp.max(s, -1, keepdims=True), sb)
    p = jnp.exp(s - m)
    return p / (jnp.sum(p, -1, keepdims=True) + jnp.exp(sb - m))


def _window_attn_latent(q, k, v, k_ctx, v_ctx, sink):
    bsz, s_len = q.shape[:2]
    c_len = k_ctx.shape[1]
    nb = s_len // BLOCK
    qb = q.reshape(bsz, nb, BLOCK, A_KV_HEADS, A_GROUP, HEAD_DIM)

    def band(t):
        tp = jnp.pad(t, ((0, 0), (BLOCK, BLOCK), (0, 0), (0, 0)))
        tp = tp.reshape(bsz, nb + 2, BLOCK, A_KV_HEADS, HEAD_DIM)
        return jnp.concatenate([tp[:, :-2], tp[:, 1:-1], tp[:, 2:]], axis=2)

    def per_block(t):
        return jnp.broadcast_to(t[:, None], (bsz, nb) + t.shape[1:])

    kw = jnp.concatenate([band(k), per_block(k_ctx)], axis=2)
    vw = jnp.concatenate([band(v), per_block(v_ctx)], axis=2)
    rel = jnp.arange(3 * BLOCK)[None, :] - BLOCK - jnp.arange(BLOCK)[:, None]
    kpos = (jnp.arange(nb)[:, None] - 1) * BLOCK + jnp.arange(3 * BLOCK)[None, :]
    lat_mask = (jnp.abs(rel) <= WINDOW)[None] & ((kpos >= 0) & (kpos < s_len))[:, None, :]
    mask = jnp.concatenate([lat_mask, jnp.ones((nb, BLOCK, c_len), bool)], axis=-1)
    s = jnp.einsum('bnqkgd,bnjkd->bnkgqj', qb, kw).astype(jnp.float32) * ATTN_SCALE
    s = jnp.where(mask[None, :, None, None], s, -jnp.inf)
    p = _sink_softmax(s, sink)
    o = jnp.einsum('bnkgqj,bnjkd->bnqkgd', p.astype(vw.dtype), vw)
    return o.reshape(bsz, s_len, A_Q)


def _global_attn_latent(q, k, v, k_ctx, v_ctx):
    bsz, s_len = q.shape[:2]
    nb = s_len // BLOCK
    k_all = jnp.concatenate([k, k_ctx], axis=1)
    v_all = jnp.concatenate([v, v_ctx], axis=1)
    qb = jnp.moveaxis(q.reshape(bsz, nb, BLOCK, B_KV_HEADS, B_GROUP, HEAD_DIM), 1, 0)

    def one_block(q_blk):
        s = jnp.einsum('bqkgd,bjkd->bkgqj', q_blk, k_all).astype(jnp.float32) * ATTN_SCALE
        p = jax.nn.softmax(s, axis=-1)
        return jnp.einsum('bkgqj,bjkd->bqkgd', p.astype(v_all.dtype), v_all)

    o = lax.map(one_block, qb)
    return jnp.moveaxis(o, 0, 1).reshape(bsz, s_len, B_Q)


def _ctx_attn(q, k, v, sink):
    bsz, c_len = q.shape[:2]
    s = jnp.einsum('bqkgd,bjkd->bkgqj', q, k).astype(jnp.float32) * ATTN_SCALE
    p = _sink_softmax(s, sink) if sink is not None else jax.nn.softmax(s, axis=-1)
    o = jnp.einsum('bkgqj,bjkd->bqkgd', p.astype(v.dtype), v)
    return o.reshape(bsz, c_len, -1)


def _merge(o_a, o_b, out_gain, w_out):
    o = jnp.concatenate([_rms_norm(o_a, out_gain[:A_Q]), _rms_norm(o_b, out_gain[A_Q:])], axis=-1)
    return jnp.einsum('bnm,md->bnd', o, w_out)


def _moe(h, w_router, b_router, w1, b1, w2, b2):
    n = h.shape[0]
    logits = (h @ w_router + b_router).astype(jnp.float32)
    top_v, top_i = lax.top_k(logits, TOP_K)
    top_w = jax.nn.softmax(top_v, axis=-1)
    gates = jnp.zeros((n, N_EXPERTS), jnp.float32).at[jnp.arange(n)[:, None], top_i].set(top_w)

    def expert(acc, xs):
        w1e, b1e, w2e, b2e, ge = xs
        u = h @ w1e + b1e
        glu = jnp.minimum(u[:, 0::2], SWIGLU_LIMIT)
        lin = jnp.clip(u[:, 1::2], -SWIGLU_LIMIT, SWIGLU_LIMIT)
        a = glu * jax.nn.sigmoid(SWIGLU_ALPHA * glu) * (lin + 1.0)
        y = a @ w2e + b2e
        return acc + ge[:, None].astype(y.dtype) * y, None

    out, _ = lax.scan(expert, jnp.zeros_like(h), (w1, b1, w2, b2, gates.T))
    return out


def setup_inputs(seed: int = 0) -> dict:
    key = jax.random.key(seed)
    ks = jax.random.split(key, 24)
    f32 = jnp.float32

    def nrm(k, shape, s):
        return s * jax.random.normal(k, shape, f32)

    v_scale = jnp.concatenate([
        jnp.ones((A_Q + A_KV,), f32), jnp.full((A_KV,), DEEPNORM_BETA, f32),
        jnp.ones((B_Q + B_KV,), f32), jnp.full((B_KV,), DEEPNORM_BETA, f32)])
    return {
        'x': nrm(ks[0], (BATCH, SEQ, D_MODEL), 1.0),
        'c': nrm(ks[1], (BATCH, D_MODEL), 1.0),
        'ctx': nrm(ks[2], (BATCH, CTX_LEN, D_MODEL), 1.0),
        'c_ctx': nrm(ks[3], (D_MODEL,), 1.0),
        'w_ada': nrm(ks[4], (DEPTH, D_MODEL, 6 * D_MODEL), 0.5 * D_MODEL ** -0.5),
        'b_ada': nrm(ks[5], (DEPTH, 6 * D_MODEL), 0.02),
        'w_in': nrm(ks[6], (DEPTH, D_MODEL, IN_COLS), D_MODEL ** -0.5) * v_scale,
        'q_gain': 1.0 + nrm(ks[7], (DEPTH, HEAD_DIM), 0.05),
        'k_gain': 1.0 + nrm(ks[8], (DEPTH, HEAD_DIM), 0.05),
        'sink': nrm(ks[9], (DEPTH, A_Q_HEADS), 0.5),
        'out_gain': 1.0 + nrm(ks[10], (DEPTH, MIX_WIDTH), 0.05),
        'w_out': nrm(ks[11], (DEPTH, MIX_WIDTH, D_MODEL), DEEPNORM_BETA * MIX_WIDTH ** -0.5),
        'ln1_g': 1.0 + nrm(ks[12], (DEPTH, D_MODEL), 0.05),
        'ln1_b': nrm(ks[13], (DEPTH, D_MODEL), 0.02),
        'w_router': nrm(ks[14], (DEPTH, D_MODEL, N_EXPERTS), D_MODEL ** -0.5),
        'b_router': nrm(ks[15], (DEPTH, N_EXPERTS), 0.01),
        'w1': nrm(ks[16], (DEPTH, N_EXPERTS, D_MODEL, 2 * D_EXPERT), D_MODEL ** -0.5),
        'b1': nrm(ks[17], (DEPTH, N_EXPERTS, 2 * D_EXPERT), 0.02),
        'w2': nrm(ks[18], (DEPTH, N_EXPERTS, D_EXPERT, D_MODEL), DEEPNORM_BETA * D_EXPERT ** -0.5),
        'b2': nrm(ks[19], (DEPTH, N_EXPERTS, D_MODEL), 0.02),
        'ln2_g': 1.0 + nrm(ks[20], (DEPTH, D_MODEL), 0.05),
        'ln2_b': nrm(ks[21], (DEPTH, D_MODEL), 0.02),
    }


def reference(x, c, ctx, c_ctx, w_ada, b_ada, w_in, q_gain, k_gain, sink, out_gain, w_out,
              ln1_g, ln1_b, w_router, b_router, w1, b1, w2, b2, ln2_g, ln2_b):
    bsz, s_len, d = x.shape
    c_len = ctx.shape[1]
    rows = s_len // GRID_W
    cos, sin = _axial_rope_tables(rows, x.dtype)
    xl, xc = x, ctx
    for l in range(DEPTH):
        last = l == DEPTH - 1
        sh1, sc1, g1, sh2, sc2, g2 = [t[:, None, :] for t in _modulation(c, w_ada[l], b_ada[l])]
        csh1, csc1, cg1, csh2, csc2, cg2 = _modulation(c_ctx, w_ada[l], b_ada[l])
        sink_l = sink[l].reshape(A_KV_HEADS, A_GROUP).astype(jnp.float32)

        h_l = _modulate(xl, sh1, sc1)
        h_c = _modulate(xc, csh1, csc1)
        qa, ka, va, qb, kb, vb = _project(h_l, w_in[l], q_gain[l], k_gain[l], cos, sin)
        cqa, cka, cva, cqb, ckb, cvb = _project(h_c, w_in[l], q_gain[l], k_gain[l], None, None)
        o_a = _window_attn_latent(qa, ka, va, cka, cva, sink_l)
        o_b = _global_attn_latent(qb, kb, vb, ckb, cvb)
        mix_l = _merge(o_a, o_b, out_gain[l], w_out[l])
        xl = _layer_norm(DEEPNORM_ALPHA * xl + g1 * mix_l, ln1_g[l], ln1_b[l])

        if last:
            h2 = _modulate(xl, sh2, sc2).reshape(bsz * s_len, d)
            y = _moe(h2, w_router[l], b_router[l], w1[l], b1[l], w2[l], b2[l]).reshape(bsz, s_len, d)
            xl = _layer_norm(DEEPNORM_ALPHA * xl + g2 * y, ln2_g[l], ln2_b[l])
        else:
            co_a = _ctx_attn(cqa, cka, cva, sink_l)
            co_b = _ctx_attn(cqb, ckb, cvb, None)
            mix_c = _merge(co_a, co_b, out_gain[l], w_out[l])
            xc = _layer_norm(DEEPNORM_ALPHA * xc + cg1 * mix_c, ln1_g[l], ln1_b[l])
            h2 = jnp.concatenate([_modulate(xl, sh2, sc2).reshape(bsz * s_len, d),
                                  _modulate(xc, csh2, csc2).reshape(bsz * c_len, d)], axis=0)
            y = _moe(h2, w_router[l], b_router[l], w1[l], b1[l], w2[l], b2[l])
            y_l = y[:bsz * s_len].reshape(bsz, s_len, d)
            y_c = y[bsz * s_len:].reshape(bsz, c_len, d)
            xl = _layer_norm(DEEPNORM_ALPHA * xl + g2 * y_l, ln2_g[l], ln2_b[l])
            xc = _layer_norm(DEEPNORM_ALPHA * xc + cg2 * y_c, ln2_g[l], ln2_b[l])
    return xl
```

```python
import functools

import jax
import jax.numpy as jnp
from jax import lax
from jax.experimental import pallas as pl
from jax.experimental.pallas import tpu as pltpu

F32 = jnp.float32
BF16 = jnp.bfloat16

GRID_W = 64
HEAD_DIM = 64
N_HEADS = 16
KV_HEADS = 2
GROUP = N_HEADS // KV_HEADS
WINDOW = 128
BLOCK = 128
ROPE_THETA = 10000.0
ATTN_SCALE = HEAD_DIM ** -0.5
A_Q = N_HEADS * HEAD_DIM
A_KV = KV_HEADS * HEAD_DIM
N_EXPERTS = 32
TOP_K = 4
D_EXPERT = 512
SWIGLU_LIMIT = 7.0
SWIGLU_ALPHA = 1.702
LN_EPS = 1e-5
RMS_EPS = 1e-6
LANES = 128
NEG_BIG = -1e30

VMEM_LIMIT = 56 * 1024 * 1024


def _cparams(sem):
    return pltpu.CompilerParams(dimension_semantics=sem, vmem_limit_bytes=VMEM_LIMIT)


def _ln(x):
    mu = jnp.mean(x, -1, keepdims=True)
    xc = x - mu
    var = jnp.mean(xc * xc, -1, keepdims=True)
    return xc * lax.rsqrt(var + LN_EPS)


def _mod_kernel(condT_ref, w_ref, b_ref, o_ref):
    ct = condT_ref[...]
    s = ct * jax.nn.sigmoid(ct)
    w = w_ref[...]
    r0 = jnp.sum(s[:, 0:1] * w, axis=0, keepdims=True)
    r1 = jnp.sum(s[:, 1:2] * w, axis=0, keepdims=True)
    o_ref[...] = jnp.concatenate([r0, r1], axis=0) + b_ref[...]


def _modulation(condT, w_ada, b_ada):
    depth, d, e = w_ada.shape
    tn = 512
    return pl.pallas_call(
        _mod_kernel,
        grid=(depth, e // tn),
        in_specs=[
            pl.BlockSpec((d, 2), lambda l, j: (0, 0)),
            pl.BlockSpec((None, d, tn), lambda l, j: (l, 0, j)),
            pl.BlockSpec((None, 1, tn), lambda l, j: (l, 0, j)),
        ],
        out_specs=pl.BlockSpec((None, 2, tn), lambda l, j: (l, 0, j)),
        out_shape=jax.ShapeDtypeStruct((depth, 2, e), F32),
        compiler_params=_cparams(("parallel", "parallel")),
        name="modulation",
    )(condT, w_ada, b_ada.reshape(depth, 1, e))


def _proj_kernel(x_ref, sh_ref, sc_ref, w_ref, gq_ref, gk_ref, bd_ref,
                 cos_ref, sa_ref, sb_ref,
                 qa_ref, kaT_ref, va_ref, qb_ref, kbT_ref, vb_ref):
    x = x_ref[...]
    h = _ln(x) * (1.0 + sc_ref[...]) + sh_ref[...]
    p = jnp.dot(h.astype(BF16), w_ref[...], preferred_element_type=F32)
    cos, sa, sb = cos_ref[...], sa_ref[...], sb_ref[...]
    lane = lax.broadcasted_iota(jnp.int32, cos.shape, 1)

    def rope(t):
        return (t * cos + pltpu.roll(t, LANES - 16, 1) * sa
                + pltpu.roll(t, 16, 1) * sb)

    def head_rms(t, width):
        sq = t * t
        hi = sq.astype(BF16)
        lo = (sq - hi.astype(F32)).astype(BF16)
        bd = bd_ref[0:width, 0:width]
        ss = (jnp.dot(hi, bd, preferred_element_type=F32)
              + jnp.dot(lo, bd, preferred_element_type=F32))
        return lax.rsqrt(ss * (1.0 / HEAD_DIM) + RMS_EPS)

    def with_ones(t, g):
        src = t if g == 0 else pltpu.roll(t, HEAD_DIM, 1)
        return jnp.where(lane < HEAD_DIM, src,
                         jnp.where(lane == HEAD_DIM, 1.0, 0.0)).astype(BF16)

    def write_q(q_out, q):
        for c in range(A_Q // LANES):
            r = rope(q[:, c * LANES:(c + 1) * LANES]) * ATTN_SCALE
            q_out[2 * c] = r[:, :HEAD_DIM].astype(BF16)
            q_out[2 * c + 1] = r[:, HEAD_DIM:].astype(BF16)

    o = 0
    write_q(qa_ref, p[:, o:o + A_Q]); o += A_Q
    kaT_ref[...] = rope(p[:, o:o + A_KV]).T.astype(BF16); o += A_KV
    va = p[:, o:o + A_KV]; o += A_KV
    va_ref[0] = with_ones(va, 0)
    va_ref[1] = with_ones(va, 1)
    qb = p[:, o:o + A_Q]; o += A_Q
    write_q(qb_ref, qb * head_rms(qb, A_Q) * gq_ref[...])
    kb = p[:, o:o + A_KV]; o += A_KV
    kbT_ref[...] = rope(kb * head_rms(kb, A_KV) * gk_ref[...]).T.astype(BF16)
    vb = p[:, o:o + A_KV]
    vb_ref[0] = with_ones(vb, 0)
    vb_ref[1] = with_ones(vb, 1)


def _project(x_all, modl, w_in_b, gq_t, gk_t, bd, cos_t, sa_t, sb_t, n_lat_rows, tm):
    nt, d = x_all.shape
    n_lat_tiles = n_lat_rows // tm
    which = lambda i: jnp.where(i >= n_lat_tiles, 1, 0)
    modspec = lambda j: pl.BlockSpec((None, 1, d), lambda i: (which(i) * 6 + j, 0, 0))
    const2 = lambda a: pl.BlockSpec(a.shape, lambda i: (0, 0))
    q_shape = jax.ShapeDtypeStruct((N_HEADS, nt, HEAD_DIM), BF16)
    kT_shape = jax.ShapeDtypeStruct((A_KV, nt), BF16)
    v_shape = jax.ShapeDtypeStruct((KV_HEADS, nt, LANES), BF16)
    q_spec = pl.BlockSpec((N_HEADS, tm, HEAD_DIM), lambda i: (0, i, 0))
    kT_spec = pl.BlockSpec((A_KV, tm), lambda i: (0, i))
    v_spec = pl.BlockSpec((KV_HEADS, tm, LANES), lambda i: (0, i, 0))
    tab_spec = pl.BlockSpec((tm, LANES), lambda i: (i, 0))
    return pl.pallas_call(
        _proj_kernel,
        grid=(nt // tm,),
        in_specs=[pl.BlockSpec((tm, d), lambda i: (i, 0)), modspec(0), modspec(1),
                  const2(w_in_b), const2(gq_t), const2(gk_t), const2(bd),
                  tab_spec, tab_spec, tab_spec],
        out_specs=[q_spec, kT_spec, v_spec, q_spec, kT_spec, v_spec],
        out_shape=[q_shape, kT_shape, v_shape, q_shape, kT_shape, v_shape],
        compiler_params=_cparams(("parallel",)),
        name="project",
    )(x_all, modl, modl, w_in_b, gq_t, gk_t, bd, cos_t, sa_t, sb_t)


def _attn_b_kernel(q_ref, kT_ref, v_ref, o_ref, m_scr, acc_scr, *, n_lat_rows, tk):
    i = pl.program_id(0)
    tq = q_ref.shape[1]
    nt = kT_ref.shape[1]
    is_ctx = i * tq >= n_lat_rows
    k_lo = jnp.where(is_ctx, n_lat_rows // tk, 0)
    m_scr[...] = jnp.full(m_scr.shape, NEG_BIG, F32)
    acc_scr[...] = jnp.zeros(acc_scr.shape, F32)

    def step(j, carry):
        off = pl.multiple_of(j * tk, tk)
        for h in range(N_HEADS):
            g = h // GROUP
            kt = kT_ref[g * HEAD_DIM:(g + 1) * HEAD_DIM, pl.ds(off, tk)]
            s = jnp.dot(q_ref[h], kt, preferred_element_type=F32)
            m_prev = m_scr[h]
            m_new = jnp.maximum(m_prev, jnp.max(s, axis=1, keepdims=True))
            alpha = jnp.exp(m_prev - m_new)
            p = jnp.exp(s - jnp.concatenate([m_new] * (tk // LANES), axis=1))
            pv = jnp.dot(p.astype(BF16), v_ref[g, pl.ds(off, tk), :],
                         preferred_element_type=F32)
            acc_scr[h] = acc_scr[h] * alpha + pv
            m_scr[h] = m_new
        return carry

    lax.fori_loop(k_lo, nt // tk, step, 0)
    for h in range(N_HEADS):
        acc = acc_scr[h]
        o_ref[:, h * HEAD_DIM:(h + 1) * HEAD_DIM] = (
            acc[:, :HEAD_DIM] / acc[:, HEAD_DIM:HEAD_DIM + 1])


def _attn_b(q_hm, kT, v2, n_q_rows, n_lat_rows, tq=128, tk=256):
    nt = kT.shape[1]
    return pl.pallas_call(
        functools.partial(_attn_b_kernel, n_lat_rows=n_lat_rows, tk=tk),
        grid=(n_q_rows // tq,),
        in_specs=[pl.BlockSpec((N_HEADS, tq, HEAD_DIM), lambda i: (0, i, 0)),
                  pl.BlockSpec((A_KV, nt), lambda i: (0, 0)),
                  pl.BlockSpec((KV_HEADS, nt, LANES), lambda i: (0, 0, 0))],
        out_specs=pl.BlockSpec((tq, A_Q), lambda i: (i, 0)),
        out_shape=jax.ShapeDtypeStruct((n_q_rows, A_Q), F32),
        scratch_shapes=[pltpu.VMEM((N_HEADS, tq, LANES), F32),
                        pltpu.VMEM((N_HEADS, tq, LANES), F32)],
        compiler_params=_cparams(("parallel",)),
        name="attn_global",
    )(q_hm, kT, v2)


def _attn_a_kernel(sink_ref, q_ref, kT_ref, v_ref, o_ref, *, n_lat_rows):
    n = pl.program_id(0)
    n_lat_blocks = n_lat_rows // BLOCK
    nt = kT_ref.shape[1]
    n_ctx = nt - n_lat_rows
    is_lat = n < n_lat_blocks
    ii = lax.broadcasted_iota(jnp.int32, (BLOCK, BLOCK), 0)
    jj = lax.broadcasted_iota(jnp.int32, (BLOCK, BLOCK), 1)
    blocks = []
    for d in (-1, 0, 1):
        blk = n + d
        valid = is_lat & (blk >= 0) & (blk < n_lat_blocks)
        off = pl.multiple_of(jnp.clip(blk, 0, n_lat_blocks - 1) * BLOCK, BLOCK)
        rel_ok = (jj >= ii) if d == -1 else ((jj <= ii) if d == 1 else (jj >= 0))
        blocks.append((off, valid & rel_ok))

    for h in range(N_HEADS):
        g = h // GROUP
        q = q_ref[h]
        kg = kT_ref.at[g * HEAD_DIM:(g + 1) * HEAD_DIM, :]
        vg = v_ref.at[g]
        sink = sink_ref[h]
        s_list = []
        for off, mask in blocks:
            s = jnp.dot(q, kg[:, pl.ds(off, BLOCK)], preferred_element_type=F32)
            s_list.append(jnp.where(mask, s, NEG_BIG))
        s_ctx = jnp.dot(q, kg[:, n_lat_rows:nt], preferred_element_type=F32)
        m = jnp.max(s_ctx, axis=1, keepdims=True)
        for s in s_list:
            m = jnp.maximum(m, jnp.max(s, axis=1, keepdims=True))
        m = jnp.maximum(m, sink)
        acc = jnp.dot(jnp.exp(s_ctx - m).astype(BF16), vg[n_lat_rows:nt, :],
                      preferred_element_type=F32)
        for (off, _), s in zip(blocks, s_list):
            acc += jnp.dot(jnp.exp(s - m).astype(BF16), vg[pl.ds(off, BLOCK), :],
                           preferred_element_type=F32)
        denom = acc[:, HEAD_DIM:HEAD_DIM + 1] + jnp.exp(sink - m)
        o_ref[:, h * HEAD_DIM:(h + 1) * HEAD_DIM] = acc[:, :HEAD_DIM] / denom


def _attn_a(sink_l, q_hm, kT, v2, n_q_rows, n_lat_rows):
    nt = kT.shape[1]
    return pl.pallas_call(
        functools.partial(_attn_a_kernel, n_lat_rows=n_lat_rows),
        grid_spec=pltpu.PrefetchScalarGridSpec(
            num_scalar_prefetch=1,
            grid=(n_q_rows // BLOCK,),
            in_specs=[pl.BlockSpec((N_HEADS, BLOCK, HEAD_DIM), lambda i, s: (0, i, 0)),
                      pl.BlockSpec((A_KV, nt), lambda i, s: (0, 0)),
                      pl.BlockSpec((KV_HEADS, nt, LANES), lambda i, s: (0, 0, 0))],
            out_specs=pl.BlockSpec((BLOCK, A_Q), lambda i, s: (i, 0)),
        ),
        out_shape=jax.ShapeDtypeStruct((n_q_rows, A_Q), F32),
        compiler_params=_cparams(("parallel",)),
        name="attn_window",
    )(sink_l, q_hm, kT, v2)


def _merge_kernel(oa_ref, ob_ref, x_ref, og_ref, wout_ref, g1_ref, lg_ref, lb_ref,
                  sh2_ref, sc2_ref, wr_ref, br_ref,
                  x1_ref, h2_ref, ti_ref, tw_ref, *, alpha):
    def rms(t, g):
        return t * lax.rsqrt(jnp.mean(t * t, -1, keepdims=True) + RMS_EPS) * g

    og = og_ref[...]
    o = jnp.concatenate([rms(oa_ref[...], og[:, :A_Q]), rms(ob_ref[...], og[:, A_Q:])], axis=1)
    mix = jnp.dot(o.astype(BF16), wout_ref[...], preferred_element_type=F32)
    x1 = _ln(alpha * x_ref[...] + g1_ref[...] * mix) * lg_ref[...] + lb_ref[...]
    x1_ref[...] = x1
    h2 = _ln(x1) * (1.0 + sc2_ref[...]) + sh2_ref[...]
    tm, d = h2.shape
    chunks = d // LANES
    for c in range(chunks):
        h2_ref[pl.ds(c, tm, stride=chunks), :] = h2[:, c * LANES:(c + 1) * LANES]

    logits = jnp.dot(h2, wr_ref[...], preferred_element_type=F32,
                     precision=lax.Precision.HIGHEST) + br_ref[...]
    lane = lax.broadcasted_iota(jnp.int32, logits.shape, 1)
    work = logits
    vals, ids = [], []
    for _ in range(TOP_K):
        mx = jnp.max(work, axis=1, keepdims=True)
        idx = jnp.min(jnp.where(work == mx, lane, LANES), axis=1, keepdims=True)
        vals.append(mx)
        ids.append(idx)
        work = jnp.where(lane == idx, NEG_BIG, work)
    es = [jnp.exp(v - vals[0]) for v in vals]
    tot = es[0] + es[1] + es[2] + es[3]
    ti = jnp.zeros(logits.shape, jnp.int32)
    tw = jnp.zeros(logits.shape, F32)
    for k in range(TOP_K):
        ti = jnp.where(lane == k, ids[k], ti)
        tw = jnp.where(lane == k, es[k] / tot, tw)
    ti_ref[...] = ti
    tw_ref[...] = tw


def _merge(o_a, o_b, x_all, modl, og, w_out_b, lg, lb, wr_p, br_p, n_rows, n_lat_rows, tm, alpha):
    d = x_all.shape[1]
    chunks = d // LANES
    n_lat_tiles = n_lat_rows // tm
    which = lambda i: jnp.where(i >= n_lat_tiles, 1, 0)
    modspec = lambda j: pl.BlockSpec((None, 1, d), lambda i: (which(i) * 6 + j, 0, 0))
    const2 = lambda a: pl.BlockSpec(a.shape, lambda i: (0, 0))
    row = lambda w: pl.BlockSpec((tm, w), lambda i: (i, 0))
    return pl.pallas_call(
        functools.partial(_merge_kernel, alpha=alpha),
        grid=(n_rows // tm,),
        in_specs=[row(A_Q), row(A_Q), row(d), const2(og), const2(w_out_b), modspec(2),
                  const2(lg), const2(lb), modspec(3), modspec(4), const2(wr_p), const2(br_p)],
        out_specs=[row(d), pl.BlockSpec((tm * chunks, LANES), lambda i: (i, 0)),
                   row(LANES), row(LANES)],
        out_shape=[jax.ShapeDtypeStruct((n_rows, d), F32),
                   jax.ShapeDtypeStruct((n_rows * chunks, LANES), F32),
                   jax.ShapeDtypeStruct((n_rows, LANES), jnp.int32),
                   jax.ShapeDtypeStruct((n_rows, LANES), F32)],
        compiler_params=_cparams(("parallel",)),
        name="merge_route",
    )(o_a, o_b, x_all, og, w_out_b, modl, lg, lb, modl, modl, wr_p, br_p)


def _expert_kernel(te_ref, tfirst_ref, tvalid_ref, src_ref, dst_ref,
                   h2_hbm, gate_ref, w1_ref, b1_ref, w2_ref, b2_ref, out_hbm,
                   gbuf, obuf, w2b, gsem, ssem, *, tm, chunks):
    i = pl.program_id(0)
    n_valid = tvalid_ref[i]

    @pl.when(n_valid > 0)
    def _():
        base = i * tm

        def row_in(r):
            s = pl.multiple_of(src_ref[base + r] * chunks, chunks)
            return pltpu.make_async_copy(h2_hbm.at[pl.ds(s, chunks), :],
                                         gbuf.at[pl.ds(r * chunks, chunks), :], gsem)

        def row_out(r):
            t = pl.multiple_of(dst_ref[base + r] * chunks, chunks)
            return pltpu.make_async_copy(obuf.at[pl.ds(r * chunks, chunks), :],
                                         out_hbm.at[pl.ds(t, chunks), :], ssem)

        def start_in(r, c):
            row_in(r).start()
            return c

        lax.fori_loop(0, tm, start_in, 0)

        @pl.when(tfirst_ref[i] > 0)
        def _():
            w2b[...] = w2_ref[...].astype(BF16)

        def wait_in(r, c):
            row_in(r).wait()
            return c

        lax.fori_loop(0, tm, wait_in, 0)
        x = jnp.concatenate([gbuf[pl.ds(c, tm, stride=chunks), :] for c in range(chunks)],
                            axis=1).astype(BF16)
        u = jnp.dot(x, w1_ref[...], preferred_element_type=F32) + b1_ref[...]
        glu = jnp.minimum(u[:, :D_EXPERT], SWIGLU_LIMIT)
        lin = jnp.clip(u[:, D_EXPERT:], -SWIGLU_LIMIT, SWIGLU_LIMIT)
        a = glu * jax.nn.sigmoid(SWIGLU_ALPHA * glu) * (lin + 1.0)
        y = jnp.dot(a.astype(BF16), w2b[...], preferred_element_type=F32) + b2_ref[...]
        y = y * gate_ref[...]
        for c in range(chunks):
            obuf[pl.ds(c, tm, stride=chunks), :] = y[:, c * LANES:(c + 1) * LANES]

        def start_out(r, c):
            row_out(r).start()
            return c

        lax.fori_loop(0, n_valid, start_out, 0)

        def wait_out(r, c):
            row_out(r).wait()
            return c

        lax.fori_loop(0, n_valid, wait_out, 0)


def _experts(meta, h2_lin, w1p, b1p, w2, b2, n_tok, d, tm):
    tile_e, tile_first, tile_valid, src, dst, gate_sorted = meta
    n_tiles = tile_e.shape[0]
    chunks = d // LANES
    out_rows = TOP_K * n_tok * chunks
    return pl.pallas_call(
        functools.partial(_expert_kernel, tm=tm, chunks=chunks),
        grid_spec=pltpu.PrefetchScalarGridSpec(
            num_scalar_prefetch=5,
            grid=(n_tiles,),
            in_specs=[
                pl.BlockSpec(memory_space=pl.ANY),
                pl.BlockSpec((tm, 1), lambda i, te, *_: (i, 0)),
                pl.BlockSpec((None, d, 2 * D_EXPERT), lambda i, te, *_: (te[i], 0, 0)),
                pl.BlockSpec((None, 1, 2 * D_EXPERT), lambda i, te, *_: (te[i], 0, 0)),
                pl.BlockSpec((None, D_EXPERT, d), lambda i, te, *_: (te[i], 0, 0)),
                pl.BlockSpec((None, 1, d), lambda i, te, *_: (te[i], 0, 0)),
            ],
            out_specs=pl.BlockSpec(memory_space=pl.ANY),
            scratch_shapes=[pltpu.VMEM((tm * chunks, LANES), F32),
                            pltpu.VMEM((tm * chunks, LANES), F32),
                            pltpu.VMEM((D_EXPERT, d), BF16),
                            pltpu.SemaphoreType.DMA, pltpu.SemaphoreType.DMA],
        ),
        out_shape=jax.ShapeDtypeStruct((out_rows, LANES), F32),
        compiler_params=_cparams(("arbitrary",)),
        name="experts",
    )(tile_e, tile_first, tile_valid, src, dst, h2_lin, gate_sorted, w1p, b1p, w2, b2)


def _moe_meta(top_i, top_w, n_tok, tm):
    n_flat = n_tok * TOP_K
    n_tiles = n_flat // tm + N_EXPERTS
    flat_e = top_i.reshape(-1)
    order = jnp.argsort(flat_e, stable=True).astype(jnp.int32)
    counts = jnp.sum((flat_e[:, None] == jnp.arange(N_EXPERTS, dtype=jnp.int32)[None, :])
                     .astype(jnp.int32), axis=0)
    tiles_per = (counts + tm - 1) // tm
    tile_end = jnp.cumsum(tiles_per)
    tile_start = tile_end - tiles_per
    row_start = jnp.cumsum(counts) - counts
    tile_ids = jnp.arange(n_tiles, dtype=jnp.int32)
    tile_used = (tile_ids < tile_end[-1]).astype(jnp.int32)
    tile_e = jnp.minimum(jnp.searchsorted(tile_end, tile_ids, side="right"),
                         N_EXPERTS - 1).astype(jnp.int32)
    tile_first = ((tile_ids == tile_start[tile_e]) & (tile_used > 0)).astype(jnp.int32)
    p = jnp.arange(n_tiles * tm, dtype=jnp.int32)
    t = p // tm
    e = tile_e[t]
    within = p - tile_start[e] * tm
    valid = (tile_used[t] > 0) & (within < counts[e])
    flat = order[jnp.clip(row_start[e] + within, 0, n_flat - 1)]
    tok = flat // TOP_K
    k = flat % TOP_K
    src = jnp.where(valid, tok, 0).astype(jnp.int32)
    dst = jnp.where(valid, k * n_tok + tok, 0).astype(jnp.int32)
    gate = jnp.where(valid, top_w.reshape(-1)[flat], 0.0).astype(F32)
    tile_valid = (jnp.clip(counts[tile_e] - (tile_ids - tile_start[tile_e]) * tm, 0, tm)
                  * tile_used).astype(jnp.int32)
    return tile_e, tile_first, tile_valid, src, dst, gate.reshape(-1, 1)


def _combine_kernel(y0_ref, y1_ref, y2_ref, y3_ref, x_ref, g2_ref, lg_ref, lb_ref, o_ref, *, alpha):
    tm, d = x_ref.shape
    chunks = d // LANES
    cols = []
    for c in range(chunks):
        sl = pl.ds(c, tm, stride=chunks)
        cols.append((y0_ref[sl, :] + y1_ref[sl, :]) + (y2_ref[sl, :] + y3_ref[sl, :]))
    y = jnp.concatenate(cols, axis=1)
    o_ref[...] = _ln(alpha * x_ref[...] + g2_ref[...] * y) * lg_ref[...] + lb_ref[...]


def _combine(y_lin, x1, modl, lg, lb, n_rows, n_tok, n_lat_rows, tm, alpha):
    d = x1.shape[1]
    chunks = d // LANES
    n_lat_tiles = n_lat_rows // tm
    tok_tiles = n_tok // tm
    which = lambda i: jnp.where(i >= n_lat_tiles, 1, 0)
    yspec = lambda k: pl.BlockSpec((tm * chunks, LANES), lambda i: (k * tok_tiles + i, 0))
    const2 = lambda a: pl.BlockSpec(a.shape, lambda i: (0, 0))
    return pl.pallas_call(
        functools.partial(_combine_kernel, alpha=alpha),
        grid=(n_rows // tm,),
        in_specs=[yspec(0), yspec(1), yspec(2), yspec(3),
                  pl.BlockSpec((tm, d), lambda i: (i, 0)),
                  pl.BlockSpec((None, 1, d), lambda i: (which(i) * 6 + 5, 0, 0)),
                  const2(lg), const2(lb)],
        out_specs=pl.BlockSpec((tm, d), lambda i: (i, 0)),
        out_shape=jax.ShapeDtypeStruct((n_rows, d), F32),
        compiler_params=_cparams(("parallel",)),
        name="combine",
    )(y_lin, y_lin, y_lin, y_lin, x1, modl, lg, lb)


def _rope_tables(s_len, n_ctx):
    rows = s_len // GRID_W
    axis_dim = HEAD_DIM // 2
    row = jnp.repeat(jnp.arange(rows, dtype=F32), GRID_W)
    col = jnp.tile(jnp.arange(GRID_W, dtype=F32), rows)
    inv = ROPE_THETA ** (-jnp.arange(0, axis_dim, 2, dtype=F32) / axis_dim)
    ar = row[:, None] * inv[None, :]
    ac = col[:, None] * inv[None, :]
    ang = jnp.concatenate([ar, ar, ac, ac], -1)
    cos = jnp.tile(jnp.cos(ang), (1, 2))
    sin = jnp.tile(jnp.sin(ang), (1, 2))
    lo = (jnp.arange(LANES) % (axis_dim) < axis_dim // 2)[None, :]
    sa = jnp.where(lo, -sin, 0.0)
    sb = jnp.where(lo, 0.0, sin)
    pad = lambda t, v: jnp.concatenate([t, jnp.full((n_ctx, LANES), v, F32)], 0)
    return pad(cos, 1.0), pad(sa, 0.0), pad(sb, 0.0)


def kernel(x, c, ctx, c_ctx, w_ada, b_ada, w_in, q_gain, k_gain, sink, out_gain, w_out,
           ln1_g, ln1_b, w_router, b_router, w1, b1, w2, b2, ln2_g, ln2_b):
    bsz, s_len, d = x.shape
    n_ctx = ctx.shape[1]
    depth = w_ada.shape[0]
    assert bsz == 1 and d % LANES == 0
    nt = s_len + n_ctx
    alpha = (2.0 * depth) ** 0.25
    tm = 256

    condT = jnp.stack([c[0], c_ctx], axis=1)
    mod = _modulation(condT, w_ada, b_ada).reshape(depth, 12, 1, d)
    cos_t, sa_t, sb_t = _rope_tables(s_len, n_ctx)
    hid = jnp.arange(A_Q) // HEAD_DIM
    bd = (hid[:, None] == hid[None, :]).astype(BF16)
    w_in_b = w_in.astype(BF16)
    w_out_b = w_out.astype(BF16)
    w1p = jnp.concatenate([w1[..., 0::2], w1[..., 1::2]], axis=-1).astype(BF16)
    b1p = jnp.concatenate([b1[..., 0::2], b1[..., 1::2]], axis=-1)[:, :, None, :]
    b2r = b2[:, :, None, :]
    wr_p = jnp.pad(w_router, ((0, 0), (0, 0), (0, LANES - N_EXPERTS)))
    br_p = jnp.pad(b_router, ((0, 0), (0, LANES - N_EXPERTS)), constant_values=NEG_BIG)

    x_all = jnp.concatenate([x[0], ctx[0]], axis=0)
    for l in range(depth):
        last = l == depth - 1
        n_rows = s_len if last else nt
        modl = mod[l]
        gq_t = jnp.tile(q_gain[l], N_HEADS)[None, :]
        gk_t = jnp.tile(k_gain[l], KV_HEADS)[None, :]
        qa, kaT, va, qb, kbT, vb = _project(x_all, modl, w_in_b[l], gq_t, gk_t, bd,
                                            cos_t, sa_t, sb_t, s_len, tm)
        o_a = _attn_a(sink[l], qa, kaT, va, n_rows, s_len)
        o_b = _attn_b(qb, kbT, vb, n_rows, s_len)
        x1, h2_lin, top_i, top_w = _merge(
            o_a, o_b, x_all, modl, out_gain[l][None, :], w_out_b[l], ln1_g[l][None, :],
            ln1_b[l][None, :], wr_p[l], br_p[l][None, :], n_rows, s_len, tm, alpha)
        meta = _moe_meta(top_i[:, :TOP_K], top_w[:, :TOP_K], n_rows, tm)
        y_lin = _experts(meta, h2_lin, w1p[l], b1p[l], w2[l], b2r[l], n_rows, d, tm)
        x_all = _combine(y_lin, x1, modl, ln2_g[l][None, :], ln2_b[l][None, :],
                         n_rows, n_rows, s_len, tm, alpha)
    return x_all[None]
```

```python
import functools

import jax
import jax.numpy as jnp
from jax import lax
from jax.experimental import pallas as pl
from jax.experimental.pallas import tpu as pltpu

F32 = jnp.float32
BF16 = jnp.bfloat16

GRID_W = 64
HEAD_DIM = 64
N_HEADS = 16
KV_HEADS = 2
GROUP = N_HEADS // KV_HEADS
WINDOW = 128
BLOCK = 128
ROPE_THETA = 10000.0
ATTN_SCALE = HEAD_DIM ** -0.5
A_Q = N_HEADS * HEAD_DIM
A_KV = KV_HEADS * HEAD_DIM
N_EXPERTS = 32
TOP_K = 4
D_EXPERT = 512
SWIGLU_LIMIT = 7.0
SWIGLU_ALPHA = 1.702
LN_EPS = 1e-5
RMS_EPS = 1e-6
LANES = 128
NEG_BIG = -1e30

VMEM_LIMIT = 56 * 1024 * 1024


def _cparams(sem):
    return pltpu.CompilerParams(dimension_semantics=sem, vmem_limit_bytes=VMEM_LIMIT)


def _ln(x):
    mu = jnp.mean(x, -1, keepdims=True)
    xc = x - mu
    var = jnp.mean(xc * xc, -1, keepdims=True)
    return xc * lax.rsqrt(var + LN_EPS)


def _mod_kernel(condT_ref, w_ref, b_ref, o_ref):
    ct = condT_ref[...]
    s = ct * jax.nn.sigmoid(ct)
    w = w_ref[...]
    r0 = jnp.sum(s[:, 0:1] * w, axis=0, keepdims=True)
    r1 = jnp.sum(s[:, 1:2] * w, axis=0, keepdims=True)
    o_ref[...] = jnp.concatenate([r0, r1], axis=0) + b_ref[...]


def _modulation(condT, w_ada, b_ada):
    depth, d, e = w_ada.shape
    tn = 512
    return pl.pallas_call(
        _mod_kernel,
        grid=(depth, e // tn),
        in_specs=[
            pl.BlockSpec((d, 2), lambda l, j: (0, 0)),
            pl.BlockSpec((None, d, tn), lambda l, j: (l, 0, j)),
            pl.BlockSpec((None, 1, tn), lambda l, j: (l, 0, j)),
        ],
        out_specs=pl.BlockSpec((None, 2, tn), lambda l, j: (l, 0, j)),
        out_shape=jax.ShapeDtypeStruct((depth, 2, e), F32),
        compiler_params=_cparams(("parallel", "parallel")),
        name="modulation",
    )(condT, w_ada, b_ada.reshape(depth, 1, e))


def _proj_kernel(x_ref, sh_ref, sc_ref, w_ref, gq_ref, gk_ref, bd_ref,
                 cos_ref, sa_ref, sb_ref,
                 qa_ref, kaT_ref, va_ref, qb_ref, kbT_ref, vb_ref):
    x = x_ref[...]
    h = _ln(x) * (1.0 + sc_ref[...]) + sh_ref[...]
    p = jnp.dot(h.astype(BF16), w_ref[...], preferred_element_type=F32)
    cos, sa, sb = cos_ref[...], sa_ref[...], sb_ref[...]
    lane = lax.broadcasted_iota(jnp.int32, cos.shape, 1)

    def rope(t):
        return (t * cos + pltpu.roll(t, LANES - 16, 1) * sa
                + pltpu.roll(t, 16, 1) * sb)

    def head_rms(t, width):
        sq = t * t
        hi = sq.astype(BF16)
        lo = (sq - hi.astype(F32)).astype(BF16)
        bd = bd_ref[0:width, 0:width]
        ss = (jnp.dot(hi, bd, preferred_element_type=F32)
              + jnp.dot(lo, bd, preferred_element_type=F32))
        return lax.rsqrt(ss * (1.0 / HEAD_DIM) + RMS_EPS)

    def with_ones(t, g):
        src = t if g == 0 else pltpu.roll(t, HEAD_DIM, 1)
        return jnp.where(lane < HEAD_DIM, src,
                         jnp.where(lane == HEAD_DIM, 1.0, 0.0)).astype(BF16)

    def write_q(q_out, q):
        for c in range(A_Q // LANES):
            r = rope(q[:, c * LANES:(c + 1) * LANES]) * ATTN_SCALE
            q_out[2 * c] = r[:, :HEAD_DIM].astype(BF16)
            q_out[2 * c + 1] = r[:, HEAD_DIM:].astype(BF16)

    o = 0
    write_q(qa_ref, p[:, o:o + A_Q]); o += A_Q
    kaT_ref[...] = rope(p[:, o:o + A_KV]).T.astype(BF16); o += A_KV
    va = p[:, o:o + A_KV]; o += A_KV
    va_ref[0] = with_ones(va, 0)
    va_ref[1] = with_ones(va, 1)
    qb = p[:, o:o + A_Q]; o += A_Q
    write_q(qb_ref, qb * head_rms(qb, A_Q) * gq_ref[...])
    kb = p[:, o:o + A_KV]; o += A_KV
    kbT_ref[...] = rope(kb * head_rms(kb, A_KV) * gk_ref[...]).T.astype(BF16)
    vb = p[:, o:o + A_KV]
    vb_ref[0] = with_ones(vb, 0)
    vb_ref[1] = with_ones(vb, 1)


def _project(x_all, modl, w_in_b, gq_t, gk_t, bd, cos_t, sa_t, sb_t, n_lat_rows, tm):
    nt, d = x_all.shape
    n_lat_tiles = n_lat_rows // tm
    which = lambda i: jnp.where(i >= n_lat_tiles, 1, 0)
    modspec = lambda j: pl.BlockSpec((None, 1, d), lambda i: (which(i) * 6 + j, 0, 0))
    const2 = lambda a: pl.BlockSpec(a.shape, lambda i: (0, 0))
    q_shape = jax.ShapeDtypeStruct((N_HEADS, nt, HEAD_DIM), BF16)
    kT_shape = jax.ShapeDtypeStruct((A_KV, nt), BF16)
    v_shape = jax.ShapeDtypeStruct((KV_HEADS, nt, LANES), BF16)
    q_spec = pl.BlockSpec((N_HEADS, tm, HEAD_DIM), lambda i: (0, i, 0))
    kT_spec = pl.BlockSpec((A_KV, tm), lambda i: (0, i))
    v_spec = pl.BlockSpec((KV_HEADS, tm, LANES), lambda i: (0, i, 0))
    tab_spec = pl.BlockSpec((tm, LANES), lambda i: (i, 0))
    return pl.pallas_call(
        _proj_kernel,
        grid=(nt // tm,),
        in_specs=[pl.BlockSpec((tm, d), lambda i: (i, 0)), modspec(0), modspec(1),
                  const2(w_in_b), const2(gq_t), const2(gk_t), const2(bd),
                  tab_spec, tab_spec, tab_spec],
        out_specs=[q_spec, kT_spec, v_spec, q_spec, kT_spec, v_spec],
        out_shape=[q_shape, kT_shape, v_shape, q_shape, kT_shape, v_shape],
        compiler_params=_cparams(("parallel",)),
        name="project",
    )(x_all, modl, modl, w_in_b, gq_t, gk_t, bd, cos_t, sa_t, sb_t)


def _attn_b_kernel(q_ref, kT_ref, v_ref, o_ref, m_scr, acc_scr, *, n_lat_rows, tk):
    i = pl.program_id(0)
    tq = q_ref.shape[1]
    nt = kT_ref.shape[1]
    is_ctx = i * tq >= n_lat_rows
    k_lo = jnp.where(is_ctx, n_lat_rows // tk, 0)
    m_scr[...] = jnp.full(m_scr.shape, NEG_BIG, F32)
    acc_scr[...] = jnp.zeros(acc_scr.shape, F32)

    def step(j, carry):
        off = pl.multiple_of(j * tk, tk)
        for h in range(N_HEADS):
            g = h // GROUP
            kt = kT_ref[g * HEAD_DIM:(g + 1) * HEAD_DIM, pl.ds(off, tk)]
            s = jnp.dot(q_ref[h], kt, preferred_element_type=F32)
            m_prev = m_scr[h]
            m_new = jnp.maximum(m_prev, jnp.max(s, axis=1, keepdims=True))
            alpha = jnp.exp(m_prev - m_new)
            p = jnp.exp(s - jnp.concatenate([m_new] * (tk // LANES), axis=1))
            pv = jnp.dot(p.astype(BF16), v_ref[g, pl.ds(off, tk), :],
                         preferred_element_type=F32)
            acc_scr[h] = acc_scr[h] * alpha + pv
            m_scr[h] = m_new
        return carry

    lax.fori_loop(k_lo, nt // tk, step, 0)
    for h in range(N_HEADS):
        acc = acc_scr[h]
        o_ref[:, h * HEAD_DIM:(h + 1) * HEAD_DIM] = (
            acc[:, :HEAD_DIM] / acc[:, HEAD_DIM:HEAD_DIM + 1])


def _attn_b(q_hm, kT, v2, n_q_rows, n_lat_rows, tq=128, tk=256):
    nt = kT.shape[1]
    return pl.pallas_call(
        functools.partial(_attn_b_kernel, n_lat_rows=n_lat_rows, tk=tk),
        grid=(n_q_rows // tq,),
        in_specs=[pl.BlockSpec((N_HEADS, tq, HEAD_DIM), lambda i: (0, i, 0)),
                  pl.BlockSpec((A_KV, nt), lambda i: (0, 0)),
                  pl.BlockSpec((KV_HEADS, nt, LANES), lambda i: (0, 0, 0))],
        out_specs=pl.BlockSpec((tq, A_Q), lambda i: (i, 0)),
        out_shape=jax.ShapeDtypeStruct((n_q_rows, A_Q), F32),
        scratch_shapes=[pltpu.VMEM((N_HEADS, tq, LANES), F32),
                        pltpu.VMEM((N_HEADS, tq, LANES), F32)],
        compiler_params=_cparams(("parallel",)),
        name="attn_global",
    )(q_hm, kT, v2)


def _attn_a_kernel(sink_ref, q_ref, kT_ref, v_ref, o_ref, *, n_lat_rows):
    n = pl.program_id(0)
    n_lat_blocks = n_lat_rows // BLOCK
    nt = kT_ref.shape[1]
    is_lat = n < n_lat_blocks
    ii = lax.broadcasted_iota(jnp.int32, (BLOCK, BLOCK), 0)
    jj = lax.broadcasted_iota(jnp.int32, (BLOCK, BLOCK), 1)
    blocks = []
    for d in (-1, 0, 1):
        blk = n + d
        valid = is_lat & (blk >= 0) & (blk < n_lat_blocks)
        off = pl.multiple_of(jnp.clip(blk, 0, n_lat_blocks - 1) * BLOCK, BLOCK)
        rel_ok = (jj >= ii) if d == -1 else ((jj <= ii) if d == 1 else (jj >= 0))
        blocks.append((off, valid & rel_ok))

    for h in range(N_HEADS):
        g = h // GROUP
        q = q_ref[h]
        kg = kT_ref.at[g * HEAD_DIM:(g + 1) * HEAD_DIM, :]
        vg = v_ref.at[g]
        sink = sink_ref[h]
        s_list = []
        for off, mask in blocks:
            s = jnp.dot(q, kg[:, pl.ds(off, BLOCK)], preferred_element_type=F32)
            s_list.append(jnp.where(mask, s, NEG_BIG))
        s_ctx = jnp.dot(q, kg[:, n_lat_rows:nt], preferred_element_type=F32)
        m = jnp.max(s_ctx, axis=1, keepdims=True)
        for s in s_list:
            m = jnp.maximum(m, jnp.max(s, axis=1, keepdims=True))
        m = jnp.maximum(m, sink)
        acc = jnp.dot(jnp.exp(s_ctx - m).astype(BF16), vg[n_lat_rows:nt, :],
                      preferred_element_type=F32)
        for (off, _), s in zip(blocks, s_list):
            acc += jnp.dot(jnp.exp(s - m).astype(BF16), vg[pl.ds(off, BLOCK), :],
                           preferred_element_type=F32)
        denom = acc[:, HEAD_DIM:HEAD_DIM + 1] + jnp.exp(sink - m)
        o_ref[:, h * HEAD_DIM:(h + 1) * HEAD_DIM] = acc[:, :HEAD_DIM] / denom


def _attn_a(sink_l, q_hm, kT, v2, n_q_rows, n_lat_rows):
    nt = kT.shape[1]
    return pl.pallas_call(
        functools.partial(_attn_a_kernel, n_lat_rows=n_lat_rows),
        grid_spec=pltpu.PrefetchScalarGridSpec(
            num_scalar_prefetch=1,
            grid=(n_q_rows // BLOCK,),
            in_specs=[pl.BlockSpec((N_HEADS, BLOCK, HEAD_DIM), lambda i, s: (0, i, 0)),
                      pl.BlockSpec((A_KV, nt), lambda i, s: (0, 0)),
                      pl.BlockSpec((KV_HEADS, nt, LANES), lambda i, s: (0, 0, 0))],
            out_specs=pl.BlockSpec((BLOCK, A_Q), lambda i, s: (i, 0)),
        ),
        out_shape=jax.ShapeDtypeStruct((n_q_rows, A_Q), F32),
        compiler_params=_cparams(("parallel",)),
        name="attn_window",
    )(sink_l, q_hm, kT, v2)


def _w1_prep_kernel(w_ref, se_ref, so_ref, o_ref):
    wb = w_ref[...].astype(BF16)
    half = D_EXPERT // 2
    for b in range(2):
        blk = wb[:, 2 * half * b:2 * half * (b + 1)]
        o_ref[:, half * b:half * (b + 1)] = jnp.dot(
            blk, se_ref[...], preferred_element_type=F32).astype(BF16)
        o_ref[:, D_EXPERT + half * b:D_EXPERT + half * (b + 1)] = jnp.dot(
            blk, so_ref[...], preferred_element_type=F32).astype(BF16)


def _w1_prep(w1):
    depth, n_e, d, n2 = w1.shape
    j = jnp.arange(D_EXPERT)[:, None]
    m = jnp.arange(D_EXPERT // 2)[None, :]
    sel_even = (j == 2 * m).astype(BF16)
    sel_odd = (j == 2 * m + 1).astype(BF16)
    const2 = lambda a: pl.BlockSpec(a.shape, lambda i: (0, 0))
    out = pl.pallas_call(
        _w1_prep_kernel,
        grid=(depth * n_e,),
        in_specs=[pl.BlockSpec((None, d, n2), lambda i: (i, 0, 0)), const2(sel_even), const2(sel_odd)],
        out_specs=pl.BlockSpec((None, d, n2), lambda i: (i, 0, 0)),
        out_shape=jax.ShapeDtypeStruct((depth * n_e, d, n2), BF16),
        compiler_params=_cparams(("parallel",)),
        name="w1_prep",
    )(w1.reshape(depth * n_e, d, n2), sel_even, sel_odd)
    return out.reshape(depth, n_e, d, n2)


def _merge_kernel(oa_ref, ob_ref, x_ref, og_ref, wout_ref, g1_ref, lg_ref, lb_ref,
                  sh2_ref, sc2_ref, wr_ref, br_ref,
                  x1_ref, h2_ref, ti_ref, tw_ref, cnt_ref, *, alpha):
    def rms(t, g):
        return t * lax.rsqrt(jnp.mean(t * t, -1, keepdims=True) + RMS_EPS) * g

    og = og_ref[...]
    o = jnp.concatenate([rms(oa_ref[...], og[:, :A_Q]), rms(ob_ref[...], og[:, A_Q:])], axis=1)
    mix = jnp.dot(o.astype(BF16), wout_ref[...], preferred_element_type=F32)
    x1 = _ln(alpha * x_ref[...] + g1_ref[...] * mix) * lg_ref[...] + lb_ref[...]
    x1_ref[...] = x1
    h2 = _ln(x1) * (1.0 + sc2_ref[...]) + sh2_ref[...]
    tm, d = h2.shape
    chunks = d // LANES
    for c in range(chunks):
        h2_ref[pl.ds(c, tm, stride=chunks), :] = h2[:, c * LANES:(c + 1) * LANES]

    logits = jnp.dot(h2, wr_ref[...], preferred_element_type=F32,
                     precision=lax.Precision.HIGHEST) + br_ref[...]
    lane = lax.broadcasted_iota(jnp.int32, logits.shape, 1)
    lane_f = lane.astype(F32)
    work = logits
    vals, ids = [], []
    for _ in range(TOP_K):
        mx = jnp.max(work, axis=1, keepdims=True)
        idx = jnp.min(jnp.where(work == mx, lane_f, float(LANES)), axis=1, keepdims=True)
        vals.append(mx)
        ids.append(idx)
        work = jnp.where(lane_f == idx, NEG_BIG, work)
    es = [jnp.exp(v - vals[0]) for v in vals]
    tot = es[0] + es[1] + es[2] + es[3]

    @pl.when(pl.program_id(0) == 0)
    def _():
        cnt_ref[...] = jnp.zeros(cnt_ref.shape, F32)

    onehot = [(lane_f == ids[k]).astype(F32) for k in range(TOP_K)]
    picked = (onehot[0] + onehot[1]) + (onehot[2] + onehot[3])
    earlier = (lax.broadcasted_iota(jnp.int32, (tm, tm), 1)
               < lax.broadcasted_iota(jnp.int32, (tm, tm), 0)).astype(BF16)
    before = jnp.dot(earlier, picked.astype(BF16), preferred_element_type=F32) + cnt_ref[...]
    ranks = [jnp.sum(onehot[k] * before, axis=1, keepdims=True) for k in range(TOP_K)]
    cnt_ref[...] += jnp.sum(picked, axis=0, keepdims=True)

    ti = jnp.zeros(logits.shape, jnp.int32)
    tw = jnp.zeros(logits.shape, F32)
    for k in range(TOP_K):
        ti = jnp.where(lane == k, ids[k].astype(jnp.int32), ti)
        ti = jnp.where(lane == TOP_K + k, ranks[k].astype(jnp.int32), ti)
        tw = jnp.where(lane == k, es[k] / tot, tw)
    ti_ref[...] = ti
    tw_ref[...] = tw


def _merge(o_a, o_b, x_all, modl, og, w_out_b, lg, lb, wr_p, br_p, n_rows, n_lat_rows, tm, alpha):
    d = x_all.shape[1]
    chunks = d // LANES
    n_lat_tiles = n_lat_rows // tm
    which = lambda i: jnp.where(i >= n_lat_tiles, 1, 0)
    modspec = lambda j: pl.BlockSpec((None, 1, d), lambda i: (which(i) * 6 + j, 0, 0))
    const2 = lambda a: pl.BlockSpec(a.shape, lambda i: (0, 0))
    row = lambda w: pl.BlockSpec((tm, w), lambda i: (i, 0))
    return pl.pallas_call(
        functools.partial(_merge_kernel, alpha=alpha),
        grid=(n_rows // tm,),
        in_specs=[row(A_Q), row(A_Q), row(d), const2(og), const2(w_out_b), modspec(2),
                  const2(lg), const2(lb), modspec(3), modspec(4), const2(wr_p), const2(br_p)],
        out_specs=[row(d), pl.BlockSpec((tm * chunks, LANES), lambda i: (i, 0)),
                   row(LANES), row(LANES), pl.BlockSpec((1, LANES), lambda i: (0, 0))],
        out_shape=[jax.ShapeDtypeStruct((n_rows, d), F32),
                   jax.ShapeDtypeStruct((n_rows * chunks, LANES), F32),
                   jax.ShapeDtypeStruct((n_rows, LANES), jnp.int32),
                   jax.ShapeDtypeStruct((n_rows, LANES), F32),
                   jax.ShapeDtypeStruct((1, LANES), F32)],
        compiler_params=_cparams(("arbitrary",)),
        name="merge_route",
    )(o_a, o_b, x_all, og, w_out_b, modl, lg, lb, modl, modl, wr_p, br_p)


def _tile_meta(cnt, n_flat, tm):
    i32 = jnp.int32
    counts = cnt[0, :N_EXPERTS].astype(i32)
    n_tiles = n_flat // tm + N_EXPERTS
    tiles_per = (counts + tm - 1) // tm
    tile_end = jnp.cumsum(tiles_per).astype(i32)
    tile_start = tile_end - tiles_per
    tid = jnp.arange(n_tiles, dtype=i32)
    used = tid < tile_end[-1]
    te = jnp.minimum(jnp.sum((tile_end[None, :] <= tid[:, None]).astype(i32), axis=1), N_EXPERTS - 1)
    onehot = te[:, None] == jnp.arange(N_EXPERTS, dtype=i32)[None, :]
    cnt_t = jnp.sum(jnp.where(onehot, counts[None, :], 0), axis=1)
    start_t = jnp.sum(jnp.where(onehot, tile_start[None, :], 0), axis=1)
    tvalid = jnp.where(used, jnp.clip(cnt_t - (tid - start_t) * tm, 0, tm), 0).astype(i32)
    tfirst = (used & (tid == start_t)).astype(i32)
    txb = jnp.where(used, tid, tile_end[-1] - 1).astype(i32)
    tzero = (tvalid < tm).astype(i32)
    return (tile_start * tm).astype(i32), te.astype(i32), tfirst, tvalid, txb, tzero


def _dispatch_kernel(eid_ref, rank_ref, pstart_ref, tzero_ref, h2_hbm, xs_hbm, zbuf, sem, zsem,
                     *, tb, tm, chunks, n_tiles):
    i = pl.program_id(0)

    @pl.when(i == 0)
    def _():
        zbuf[...] = jnp.zeros(zbuf.shape, F32)

        def fill(t):
            off = pl.multiple_of(t * (tm * chunks), tm * chunks)
            return pltpu.make_async_copy(zbuf, xs_hbm.at[pl.ds(off, tm * chunks), :], zsem)

        def start(t, c):
            @pl.when(tzero_ref[t] > 0)
            def _():
                fill(t).start()
            return c

        def wait(t, c):
            @pl.when(tzero_ref[t] > 0)
            def _():
                fill(t).wait()
            return c

        lax.fori_loop(0, n_tiles, start, 0)
        lax.fori_loop(0, n_tiles, wait, 0)

    def row(j):
        flat = i * (tb * TOP_K) + j
        s = pl.multiple_of((flat // TOP_K) * chunks, chunks)
        t = pl.multiple_of((pstart_ref[eid_ref[flat]] + rank_ref[flat]) * chunks, chunks)
        return pltpu.make_async_copy(h2_hbm.at[pl.ds(s, chunks), :],
                                     xs_hbm.at[pl.ds(t, chunks), :], sem)

    def start_row(j, c):
        row(j).start()
        return c

    def wait_row(j, c):
        row(j).wait()
        return c

    lax.fori_loop(0, tb * TOP_K, start_row, 0, unroll=8)
    lax.fori_loop(0, tb * TOP_K, wait_row, 0, unroll=8)


def _dispatch(eid, rank, pstart, tzero, h2_lin, n_tok, d, tm, tb=256):
    chunks = d // LANES
    n_tiles = tzero.shape[0]
    return pl.pallas_call(
        functools.partial(_dispatch_kernel, tb=tb, tm=tm, chunks=chunks, n_tiles=n_tiles),
        grid_spec=pltpu.PrefetchScalarGridSpec(
            num_scalar_prefetch=4,
            grid=(n_tok // tb,),
            in_specs=[pl.BlockSpec(memory_space=pl.ANY)],
            out_specs=pl.BlockSpec(memory_space=pl.ANY),
            scratch_shapes=[pltpu.VMEM((tm * chunks, LANES), F32),
                            pltpu.SemaphoreType.DMA, pltpu.SemaphoreType.DMA],
        ),
        out_shape=jax.ShapeDtypeStruct((n_tiles * tm * chunks, LANES), F32),
        compiler_params=_cparams(("arbitrary",)),
        name="dispatch",
    )(eid, rank, pstart, tzero, h2_lin)


def _expert_kernel(te_ref, tfirst_ref, tvalid_ref, txb_ref,
                   x_ref, w1_ref, b1_ref, w2_ref, b2_ref, y_ref, w2b, *, tm, chunks):
    i = pl.program_id(0)

    @pl.when(tvalid_ref[i] > 0)
    def _():
        @pl.when(tfirst_ref[i] > 0)
        def _():
            w2b[...] = w2_ref[...].astype(BF16)

        x = jnp.concatenate([x_ref[pl.ds(c, tm, stride=chunks), :] for c in range(chunks)],
                            axis=1).astype(BF16)
        u = jnp.dot(x, w1_ref[...], preferred_element_type=F32) + b1_ref[...]
        glu = jnp.minimum(u[:, :D_EXPERT], SWIGLU_LIMIT)
        lin = jnp.clip(u[:, D_EXPERT:], -SWIGLU_LIMIT, SWIGLU_LIMIT)
        a = glu * jax.nn.sigmoid(SWIGLU_ALPHA * glu) * (lin + 1.0)
        y = jnp.dot(a.astype(BF16), w2b[...], preferred_element_type=F32) + b2_ref[...]
        for c in range(chunks):
            y_ref[pl.ds(c, tm, stride=chunks), :] = y[:, c * LANES:(c + 1) * LANES]

    @pl.when(tvalid_ref[i] == 0)
    def _():
        y_ref[...] = jnp.zeros(y_ref.shape, F32)


def _experts(te, tfirst, tvalid, txb, xs_lin, w1p, b1p, w2, b2, d, tm):
    n_tiles = te.shape[0]
    chunks = d // LANES
    return pl.pallas_call(
        functools.partial(_expert_kernel, tm=tm, chunks=chunks),
        grid_spec=pltpu.PrefetchScalarGridSpec(
            num_scalar_prefetch=4,
            grid=(n_tiles,),
            in_specs=[
                pl.BlockSpec((tm * chunks, LANES), lambda i, te, tf, tv, txb: (txb[i], 0)),
                pl.BlockSpec((None, d, 2 * D_EXPERT), lambda i, te, *_: (te[i], 0, 0)),
                pl.BlockSpec((None, 1, 2 * D_EXPERT), lambda i, te, *_: (te[i], 0, 0)),
                pl.BlockSpec((None, D_EXPERT, d), lambda i, te, *_: (te[i], 0, 0)),
                pl.BlockSpec((None, 1, d), lambda i, te, *_: (te[i], 0, 0)),
            ],
            out_specs=pl.BlockSpec((tm * chunks, LANES), lambda i, *_: (i, 0)),
            scratch_shapes=[pltpu.VMEM((D_EXPERT, d), BF16)],
        ),
        out_shape=jax.ShapeDtypeStruct((n_tiles * tm * chunks, LANES), F32),
        compiler_params=_cparams(("arbitrary",)),
        name="experts",
    )(te, tfirst, tvalid, txb, xs_lin, w1p, b1p, w2, b2)


def _combine_kernel(eid_ref, rank_ref, pstart_ref, ys_hbm, tw_ref, x_ref, g2_ref, lg_ref, lb_ref,
                    o_ref, gbuf, sem, *, alpha, tm, chunks):
    i = pl.program_id(0)
    n = pl.num_programs(0)
    slot = i % 2

    def row(step, slot_, j):
        flat = step * (tm * TOP_K) + j
        s = pl.multiple_of((pstart_ref[eid_ref[flat]] + rank_ref[flat]) * chunks, chunks)
        r = pl.multiple_of((j // TOP_K) * chunks, chunks)
        return pltpu.make_async_copy(ys_hbm.at[pl.ds(s, chunks), :],
                                     gbuf.at[slot_, j % TOP_K, pl.ds(r, chunks), :], sem.at[slot_])

    def issue(step, slot_):
        def body(j, c):
            row(step, slot_, j).start()
            return c
        lax.fori_loop(0, tm * TOP_K, body, 0, unroll=8)

    @pl.when(i == 0)
    def _():
        issue(0, 0)

    @pl.when(i + 1 < n)
    def _():
        issue(i + 1, 1 - slot)

    def wait_body(j, c):
        row(i, slot, j).wait()
        return c

    lax.fori_loop(0, tm * TOP_K, wait_body, 0, unroll=8)

    tw = tw_ref[...]
    cols = []
    for c in range(chunks):
        sl = pl.ds(c, tm, stride=chunks)
        acc = tw[:, 0:1] * gbuf[slot, 0, sl, :]
        for k in range(1, TOP_K):
            acc += tw[:, k:k + 1] * gbuf[slot, k, sl, :]
        cols.append(acc)
    y = jnp.concatenate(cols, axis=1)
    o_ref[...] = _ln(alpha * x_ref[...] + g2_ref[...] * y) * lg_ref[...] + lb_ref[...]


def _combine(eid, rank, pstart, ys_lin, top_w, x1, modl, lg, lb, n_rows, n_lat_rows, tm, alpha):
    d = x1.shape[1]
    chunks = d // LANES
    n_lat_tiles = n_lat_rows // tm
    which = lambda i: jnp.where(i >= n_lat_tiles, 1, 0)
    const2 = lambda a: pl.BlockSpec(a.shape, lambda i, *_: (0, 0))
    return pl.pallas_call(
        functools.partial(_combine_kernel, alpha=alpha, tm=tm, chunks=chunks),
        grid_spec=pltpu.PrefetchScalarGridSpec(
            num_scalar_prefetch=3,
            grid=(n_rows // tm,),
            in_specs=[pl.BlockSpec(memory_space=pl.ANY),
                      pl.BlockSpec((tm, LANES), lambda i, *_: (i, 0)),
                      pl.BlockSpec((tm, d), lambda i, *_: (i, 0)),
                      pl.BlockSpec((None, 1, d), lambda i, *_: (which(i) * 6 + 5, 0, 0)),
                      const2(lg), const2(lb)],
            out_specs=pl.BlockSpec((tm, d), lambda i, *_: (i, 0)),
            scratch_shapes=[pltpu.VMEM((2, TOP_K, tm * chunks, LANES), F32),
                            pltpu.SemaphoreType.DMA((2,))],
        ),
        out_shape=jax.ShapeDtypeStruct((n_rows, d), F32),
        compiler_params=_cparams(("arbitrary",)),
        name="combine",
    )(eid, rank, pstart, ys_lin, top_w, x1, modl, lg, lb)


def _rope_tables(s_len, n_ctx):
    rows = s_len // GRID_W
    axis_dim = HEAD_DIM // 2
    row = jnp.repeat(jnp.arange(rows, dtype=F32), GRID_W)
    col = jnp.tile(jnp.arange(GRID_W, dtype=F32), rows)
    inv = ROPE_THETA ** (-jnp.arange(0, axis_dim, 2, dtype=F32) / axis_dim)
    ar = row[:, None] * inv[None, :]
    ac = col[:, None] * inv[None, :]
    ang = jnp.concatenate([ar, ar, ac, ac], -1)
    cos = jnp.tile(jnp.cos(ang), (1, 2))
    sin = jnp.tile(jnp.sin(ang), (1, 2))
    lo = (jnp.arange(LANES) % (axis_dim) < axis_dim // 2)[None, :]
    sa = jnp.where(lo, -sin, 0.0)
    sb = jnp.where(lo, 0.0, sin)
    pad = lambda t, v: jnp.concatenate([t, jnp.full((n_ctx, LANES), v, F32)], 0)
    return pad(cos, 1.0), pad(sa, 0.0), pad(sb, 0.0)


def kernel(x, c, ctx, c_ctx, w_ada, b_ada, w_in, q_gain, k_gain, sink, out_gain, w_out,
           ln1_g, ln1_b, w_router, b_router, w1, b1, w2, b2, ln2_g, ln2_b):
    bsz, s_len, d = x.shape
    n_ctx = ctx.shape[1]
    depth = w_ada.shape[0]
    assert bsz == 1 and d % LANES == 0
    nt = s_len + n_ctx
    alpha = (2.0 * depth) ** 0.25
    tm = 256

    condT = jnp.stack([c[0], c_ctx], axis=1)
    mod = _modulation(condT, w_ada, b_ada).reshape(depth, 12, 1, d)
    cos_t, sa_t, sb_t = _rope_tables(s_len, n_ctx)
    hid = jnp.arange(A_Q) // HEAD_DIM
    bd = (hid[:, None] == hid[None, :]).astype(BF16)
    w_in_b = w_in.astype(BF16)
    w_out_b = w_out.astype(BF16)
    w1p = _w1_prep(w1)
    b1p = jnp.concatenate([b1[..., 0::2], b1[..., 1::2]], axis=-1)[:, :, None, :]
    b2r = b2[:, :, None, :]
    wr_p = jnp.pad(w_router, ((0, 0), (0, 0), (0, LANES - N_EXPERTS)))
    br_p = jnp.pad(b_router, ((0, 0), (0, LANES - N_EXPERTS)), constant_values=NEG_BIG)

    x_all = jnp.concatenate([x[0], ctx[0]], axis=0)
    for l in range(depth):
        last = l == depth - 1
        n_rows = s_len if last else nt
        modl = mod[l]
        gq_t = jnp.tile(q_gain[l], N_HEADS)[None, :]
        gk_t = jnp.tile(k_gain[l], KV_HEADS)[None, :]
        qa, kaT, va, qb, kbT, vb = _project(x_all, modl, w_in_b[l], gq_t, gk_t, bd,
                                            cos_t, sa_t, sb_t, s_len, tm)
        o_a = _attn_a(sink[l], qa, kaT, va, n_rows, s_len)
        o_b = _attn_b(qb, kbT, vb, n_rows, s_len)
        x1, h2_lin, top_i, top_w, cnt = _merge(
            o_a, o_b, x_all, modl, out_gain[l][None, :], w_out_b[l], ln1_g[l][None, :],
            ln1_b[l][None, :], wr_p[l], br_p[l][None, :], n_rows, s_len, tm, alpha)
        pstart, te, tfirst, tvalid, txb, tzero = _tile_meta(cnt, n_rows * TOP_K, tm)
        eid = top_i[:, :TOP_K].reshape(-1)
        rank = top_i[:, TOP_K:2 * TOP_K].reshape(-1)
        xs_lin = _dispatch(eid, rank, pstart, tzero, h2_lin, n_rows, d, tm)
        ys_lin = _experts(te, tfirst, tvalid, txb, xs_lin, w1p[l], b1p[l], w2[l], b2r[l], d, tm)
        x_all = _combine(eid, rank, pstart, ys_lin, top_w, x1, modl, ln2_g[l][None, :],
                         ln2_b[l][None, :], n_rows, s_len, tm, alpha)
    return x_all[None]
```

```python
import functools

import jax
import jax.numpy as jnp
from jax import lax
from jax.experimental import pallas as pl
from jax.experimental.pallas import tpu as pltpu

F32 = jnp.float32
BF16 = jnp.bfloat16

GRID_W = 64
HEAD_DIM = 64
N_HEADS = 16
KV_HEADS = 2
GROUP = N_HEADS // KV_HEADS
WINDOW = 128
BLOCK = 128
ROPE_THETA = 10000.0
ATTN_SCALE = HEAD_DIM ** -0.5
A_Q = N_HEADS * HEAD_DIM
A_KV = KV_HEADS * HEAD_DIM
N_EXPERTS = 32
TOP_K = 4
D_EXPERT = 512
SWIGLU_LIMIT = 7.0
SWIGLU_ALPHA = 1.702
LN_EPS = 1e-5
RMS_EPS = 1e-6
LOG2_E = 1.4426950408889634
LANES = 128
NEG_BIG = -1e30

VMEM_LIMIT = 56 * 1024 * 1024


def _cparams(sem):
    return pltpu.CompilerParams(dimension_semantics=sem, vmem_limit_bytes=VMEM_LIMIT)


def _ln(x):
    mu = jnp.mean(x, -1, keepdims=True)
    xc = x - mu
    var = jnp.mean(xc * xc, -1, keepdims=True)
    return xc * lax.rsqrt(var + LN_EPS)


def _mod_kernel(condT_ref, w_ref, b_ref, o_ref):
    ct = condT_ref[...]
    s = ct * jax.nn.sigmoid(ct)
    w = w_ref[...]
    r0 = jnp.sum(s[:, 0:1] * w, axis=0, keepdims=True)
    r1 = jnp.sum(s[:, 1:2] * w, axis=0, keepdims=True)
    o_ref[...] = jnp.concatenate([r0, r1], axis=0) + b_ref[...]


def _modulation(condT, w_ada, b_ada):
    depth, d, e = w_ada.shape
    tn = 512
    return pl.pallas_call(
        _mod_kernel,
        grid=(depth, e // tn),
        in_specs=[
            pl.BlockSpec((d, 2), lambda l, j: (0, 0)),
            pl.BlockSpec((None, d, tn), lambda l, j: (l, 0, j)),
            pl.BlockSpec((None, 1, tn), lambda l, j: (l, 0, j)),
        ],
        out_specs=pl.BlockSpec((None, 2, tn), lambda l, j: (l, 0, j)),
        out_shape=jax.ShapeDtypeStruct((depth, 2, e), F32),
        compiler_params=_cparams(("parallel", "parallel")),
        name="modulation",
    )(condT, w_ada, b_ada.reshape(depth, 1, e))


def _proj_kernel(x_ref, sh_ref, sc_ref, w_ref, gq_ref, gk_ref, bd_ref,
                 cos_ref, sa_ref, sb_ref,
                 qa_ref, kaT_ref, va_ref, qb_ref, kbT_ref, vb_ref):
    x = x_ref[...]
    h = _ln(x) * (1.0 + sc_ref[...]) + sh_ref[...]
    p = jnp.dot(h.astype(BF16), w_ref[...], preferred_element_type=F32)
    cos, sa, sb = cos_ref[...], sa_ref[...], sb_ref[...]
    lane = lax.broadcasted_iota(jnp.int32, cos.shape, 1)

    def rope(t):
        return (t * cos + pltpu.roll(t, LANES - 16, 1) * sa
                + pltpu.roll(t, 16, 1) * sb)

    def head_rms(t, width):
        sq = t * t
        hi = sq.astype(BF16)
        lo = (sq - hi.astype(F32)).astype(BF16)
        bd = bd_ref[0:width, 0:width]
        ss = (jnp.dot(hi, bd, preferred_element_type=F32)
              + jnp.dot(lo, bd, preferred_element_type=F32))
        return lax.rsqrt(ss * (1.0 / HEAD_DIM) + RMS_EPS)

    def with_ones(t, g):
        src = t if g == 0 else pltpu.roll(t, HEAD_DIM, 1)
        return jnp.where(lane < HEAD_DIM, src,
                         jnp.where(lane == HEAD_DIM, 1.0, 0.0)).astype(BF16)

    def write_q(q_out, q, scale):
        for c in range(A_Q // LANES):
            r = rope(q[:, c * LANES:(c + 1) * LANES]) * scale
            q_out[2 * c] = r[:, :HEAD_DIM].astype(BF16)
            q_out[2 * c + 1] = r[:, HEAD_DIM:].astype(BF16)

    o = 0
    write_q(qa_ref, p[:, o:o + A_Q], ATTN_SCALE); o += A_Q
    kaT_ref[...] = rope(p[:, o:o + A_KV]).T.astype(BF16); o += A_KV
    va = p[:, o:o + A_KV]; o += A_KV
    va_ref[0] = with_ones(va, 0)
    va_ref[1] = with_ones(va, 1)
    qb = p[:, o:o + A_Q]; o += A_Q
    write_q(qb_ref, qb * head_rms(qb, A_Q) * gq_ref[...], ATTN_SCALE * LOG2_E)
    kb = p[:, o:o + A_KV]; o += A_KV
    kbT_ref[...] = rope(kb * head_rms(kb, A_KV) * gk_ref[...]).T.astype(BF16)
    vb = p[:, o:o + A_KV]
    vb_ref[0] = with_ones(vb, 0)
    vb_ref[1] = with_ones(vb, 1)


def _project(x_all, modl, w_in_b, layer, gq_t, gk_t, bd, cos_t, sa_t, sb_t, n_lat_rows, tm):
    nt, d = x_all.shape
    n_lat_tiles = n_lat_rows // tm
    which = lambda i: jnp.where(i >= n_lat_tiles, 1, 0)
    modspec = lambda j: pl.BlockSpec((None, 1, d), lambda i: (which(i) * 6 + j, 0, 0))
    const2 = lambda a: pl.BlockSpec(a.shape, lambda i: (0, 0))
    layer3 = lambda a: pl.BlockSpec((None,) + a.shape[1:], lambda i: (layer, 0, 0))
    q_shape = jax.ShapeDtypeStruct((N_HEADS, nt, HEAD_DIM), BF16)
    kT_shape = jax.ShapeDtypeStruct((A_KV, nt), BF16)
    v_shape = jax.ShapeDtypeStruct((KV_HEADS, nt, LANES), BF16)
    q_spec = pl.BlockSpec((N_HEADS, tm, HEAD_DIM), lambda i: (0, i, 0))
    kT_spec = pl.BlockSpec((A_KV, tm), lambda i: (0, i))
    v_spec = pl.BlockSpec((KV_HEADS, tm, LANES), lambda i: (0, i, 0))
    tab_spec = pl.BlockSpec((tm, LANES), lambda i: (i, 0))
    return pl.pallas_call(
        _proj_kernel,
        grid=(nt // tm,),
        in_specs=[pl.BlockSpec((tm, d), lambda i: (i, 0)), modspec(0), modspec(1),
                  layer3(w_in_b), const2(gq_t), const2(gk_t), const2(bd),
                  tab_spec, tab_spec, tab_spec],
        out_specs=[q_spec, kT_spec, v_spec, q_spec, kT_spec, v_spec],
        out_shape=[q_shape, kT_shape, v_shape, q_shape, kT_shape, v_shape],
        compiler_params=_cparams(("parallel",)),
        name="project",
    )(x_all, modl, modl, w_in_b, gq_t, gk_t, bd, cos_t, sa_t, sb_t)


def _attn_b_kernel(q_ref, kT_ref, v_ref, o_ref, m_scr, acc_scr, *, n_lat_rows, tk, blocks_per_step):
    i = pl.program_id(0)
    tq = q_ref.shape[1]
    nt = kT_ref.shape[1]
    is_ctx = i * tq >= n_lat_rows
    m_scr[...] = jnp.full(m_scr.shape, NEG_BIG, F32)
    acc_scr[...] = jnp.zeros(acc_scr.shape, F32)

    def keys(off, width):
        for h in range(N_HEADS):
            g = h // GROUP
            kt = kT_ref[g * HEAD_DIM:(g + 1) * HEAD_DIM, pl.ds(off, width)]
            s = jnp.dot(q_ref[h], kt, preferred_element_type=F32)
            m_prev = m_scr[h]
            m_new = jnp.maximum(m_prev, jnp.max(s, axis=1, keepdims=True))
            alpha = jnp.exp2(m_prev - m_new)
            p = jnp.exp2(s - jnp.concatenate([m_new] * (width // LANES), axis=1))
            pv = jnp.dot(p.astype(BF16), v_ref[g, pl.ds(off, width), :],
                         preferred_element_type=F32)
            acc_scr[h] = acc_scr[h] * alpha + pv
            m_scr[h] = m_new

    def step(j, carry):
        for u in range(blocks_per_step):
            keys(pl.multiple_of(j * (tk * blocks_per_step) + u * tk, tk), tk)
        return carry

    lax.fori_loop(0, jnp.where(is_ctx, 0, n_lat_rows // (tk * blocks_per_step)), step, 0)
    keys(n_lat_rows, nt - n_lat_rows)
    for h in range(N_HEADS):
        acc = acc_scr[h]
        o_ref[:, h * HEAD_DIM:(h + 1) * HEAD_DIM] = (
            acc[:, :HEAD_DIM] / acc[:, HEAD_DIM:HEAD_DIM + 1])


def _attn_b(q_hm, kT, v2, n_q_rows, n_lat_rows, tq=128, tk=256, blocks_per_step=2):
    nt = kT.shape[1]
    assert n_lat_rows % (tk * blocks_per_step) == 0
    return pl.pallas_call(
        functools.partial(_attn_b_kernel, n_lat_rows=n_lat_rows, tk=tk,
                          blocks_per_step=blocks_per_step),
        grid=(n_q_rows // tq,),
        in_specs=[pl.BlockSpec((N_HEADS, tq, HEAD_DIM), lambda i: (0, i, 0)),
                  pl.BlockSpec((A_KV, nt), lambda i: (0, 0)),
                  pl.BlockSpec((KV_HEADS, nt, LANES), lambda i: (0, 0, 0))],
        out_specs=pl.BlockSpec((tq, A_Q), lambda i: (i, 0)),
        out_shape=jax.ShapeDtypeStruct((n_q_rows, A_Q), F32),
        scratch_shapes=[pltpu.VMEM((N_HEADS, tq, LANES), F32),
                        pltpu.VMEM((N_HEADS, tq, LANES), F32)],
        compiler_params=_cparams(("parallel",)),
        name="attn_global",
    )(q_hm, kT, v2)


def _attn_a_kernel(sink_ref, q_ref, kT_ref, v_ref, o_ref, *, n_lat_rows):
    n = pl.program_id(0)
    n_lat_blocks = n_lat_rows // BLOCK
    nt = kT_ref.shape[1]
    is_lat = n < n_lat_blocks
    ii = lax.broadcasted_iota(jnp.int32, (BLOCK, BLOCK), 0)
    jj = lax.broadcasted_iota(jnp.int32, (BLOCK, BLOCK), 1)
    blocks = []
    for d in (-1, 0, 1):
        blk = n + d
        valid = is_lat & (blk >= 0) & (blk < n_lat_blocks)
        off = pl.multiple_of(jnp.clip(blk, 0, n_lat_blocks - 1) * BLOCK, BLOCK)
        rel_ok = (jj >= ii) if d == -1 else ((jj <= ii) if d == 1 else (jj >= 0))
        blocks.append((off, valid & rel_ok))

    for h in range(N_HEADS):
        g = h // GROUP
        q = q_ref[h]
        kg = kT_ref.at[g * HEAD_DIM:(g + 1) * HEAD_DIM, :]
        vg = v_ref.at[g]
        sink = sink_ref[h]
        s_list = []
        for off, mask in blocks:
            s = jnp.dot(q, kg[:, pl.ds(off, BLOCK)], preferred_element_type=F32)
            s_list.append(jnp.where(mask, s, NEG_BIG))
        s_ctx = jnp.dot(q, kg[:, n_lat_rows:nt], preferred_element_type=F32)
        m = jnp.max(s_ctx, axis=1, keepdims=True)
        for s in s_list:
            m = jnp.maximum(m, jnp.max(s, axis=1, keepdims=True))
        m = jnp.maximum(m, sink)
        acc = jnp.dot(jnp.exp(s_ctx - m).astype(BF16), vg[n_lat_rows:nt, :],
                      preferred_element_type=F32)
        for (off, _), s in zip(blocks, s_list):
            acc += jnp.dot(jnp.exp(s - m).astype(BF16), vg[pl.ds(off, BLOCK), :],
                           preferred_element_type=F32)
        denom = acc[:, HEAD_DIM:HEAD_DIM + 1] + jnp.exp(sink - m)
        o_ref[:, h * HEAD_DIM:(h + 1) * HEAD_DIM] = acc[:, :HEAD_DIM] / denom


def _attn_a(sink_l, q_hm, kT, v2, n_q_rows, n_lat_rows):
    nt = kT.shape[1]
    return pl.pallas_call(
        functools.partial(_attn_a_kernel, n_lat_rows=n_lat_rows),
        grid_spec=pltpu.PrefetchScalarGridSpec(
            num_scalar_prefetch=1,
            grid=(n_q_rows // BLOCK,),
            in_specs=[pl.BlockSpec((N_HEADS, BLOCK, HEAD_DIM), lambda i, s: (0, i, 0)),
                      pl.BlockSpec((A_KV, nt), lambda i, s: (0, 0)),
                      pl.BlockSpec((KV_HEADS, nt, LANES), lambda i, s: (0, 0, 0))],
            out_specs=pl.BlockSpec((BLOCK, A_Q), lambda i, s: (i, 0)),
        ),
        out_shape=jax.ShapeDtypeStruct((n_q_rows, A_Q), F32),
        compiler_params=_cparams(("parallel",)),
        name="attn_window",
    )(sink_l, q_hm, kT, v2)


def _w1_prep_kernel(w_ref, se_ref, so_ref, o_ref):
    wb = w_ref[...].astype(BF16)
    half = D_EXPERT // 2
    for b in range(2):
        blk = wb[:, 2 * half * b:2 * half * (b + 1)]
        o_ref[:, half * b:half * (b + 1)] = jnp.dot(
            blk, se_ref[...], preferred_element_type=F32).astype(BF16)
        o_ref[:, D_EXPERT + half * b:D_EXPERT + half * (b + 1)] = jnp.dot(
            blk, so_ref[...], preferred_element_type=F32).astype(BF16)


def _w1_prep(w1):
    depth, n_e, d, n2 = w1.shape
    j = jnp.arange(D_EXPERT)[:, None]
    m = jnp.arange(D_EXPERT // 2)[None, :]
    sel_even = (j == 2 * m).astype(BF16)
    sel_odd = (j == 2 * m + 1).astype(BF16)
    const2 = lambda a: pl.BlockSpec(a.shape, lambda i: (0, 0))
    out = pl.pallas_call(
        _w1_prep_kernel,
        grid=(depth * n_e,),
        in_specs=[pl.BlockSpec((None, d, n2), lambda i: (i, 0, 0)), const2(sel_even), const2(sel_odd)],
        out_specs=pl.BlockSpec((None, d, n2), lambda i: (i, 0, 0)),
        out_shape=jax.ShapeDtypeStruct((depth * n_e, d, n2), BF16),
        compiler_params=_cparams(("parallel",)),
        name="w1_prep",
    )(w1.reshape(depth * n_e, d, n2), sel_even, sel_odd)
    return out


def _merge_kernel(oa_ref, ob_ref, x_ref, og_ref, wout_ref, g1_ref, lg_ref, lb_ref,
                  sh2_ref, sc2_ref, wr_ref, br_ref,
                  x1_ref, h2_ref, ti_ref, tw_ref, cnt_ref, *, alpha):
    def rms(t, g):
        return t * lax.rsqrt(jnp.mean(t * t, -1, keepdims=True) + RMS_EPS) * g

    og = og_ref[...]
    o = jnp.concatenate([rms(oa_ref[...], og[:, :A_Q]), rms(ob_ref[...], og[:, A_Q:])], axis=1)
    mix = jnp.dot(o.astype(BF16), wout_ref[...], preferred_element_type=F32)
    x1 = _ln(alpha * x_ref[...] + g1_ref[...] * mix) * lg_ref[...] + lb_ref[...]
    x1_ref[...] = x1
    h2 = _ln(x1) * (1.0 + sc2_ref[...]) + sh2_ref[...]
    tm, d = h2.shape
    chunks = d // LANES
    for c in range(chunks):
        h2_ref[pl.ds(c, tm, stride=chunks), :] = h2[:, c * LANES:(c + 1) * LANES]

    logits = jnp.dot(h2, wr_ref[...], preferred_element_type=F32,
                     precision=lax.Precision.HIGHEST) + br_ref[...]
    lane = lax.broadcasted_iota(jnp.int32, logits.shape, 1)
    lane_f = lane.astype(F32)
    work = logits
    vals, ids = [], []
    for _ in range(TOP_K):
        mx = jnp.max(work, axis=1, keepdims=True)
        idx = jnp.min(jnp.where(work == mx, lane_f, float(LANES)), axis=1, keepdims=True)
        vals.append(mx)
        ids.append(idx)
        work = jnp.where(lane_f == idx, NEG_BIG, work)
    es = [jnp.exp(v - vals[0]) for v in vals]
    tot = es[0] + es[1] + es[2] + es[3]

    @pl.when(pl.program_id(0) == 0)
    def _():
        cnt_ref[...] = jnp.zeros(cnt_ref.shape, F32)

    onehot = [(lane_f == ids[k]).astype(F32) for k in range(TOP_K)]
    picked = (onehot[0] + onehot[1]) + (onehot[2] + onehot[3])
    earlier = (lax.broadcasted_iota(jnp.int32, (tm, tm), 1)
               < lax.broadcasted_iota(jnp.int32, (tm, tm), 0)).astype(BF16)
    before = jnp.dot(earlier, picked.astype(BF16), preferred_element_type=F32) + cnt_ref[...]
    ranks = [jnp.sum(onehot[k] * before, axis=1, keepdims=True) for k in range(TOP_K)]
    cnt_ref[...] += jnp.sum(picked, axis=0, keepdims=True)

    ti = jnp.zeros(logits.shape, jnp.int32)
    tw = jnp.zeros(logits.shape, F32)
    for k in range(TOP_K):
        ti = jnp.where(lane == k, ids[k].astype(jnp.int32), ti)
        ti = jnp.where(lane == TOP_K + k, ranks[k].astype(jnp.int32), ti)
        tw = jnp.where(lane == k, es[k] / tot, tw)
    ti_ref[...] = ti
    tw_ref[...] = tw


def _merge(o_a, o_b, x_all, modl, og, w_out_b, layer, lg, lb, wr_p, br_p, n_rows, n_lat_rows, tm,
           alpha):
    d = x_all.shape[1]
    chunks = d // LANES
    n_lat_tiles = n_lat_rows // tm
    which = lambda i: jnp.where(i >= n_lat_tiles, 1, 0)
    modspec = lambda j: pl.BlockSpec((None, 1, d), lambda i: (which(i) * 6 + j, 0, 0))
    const2 = lambda a: pl.BlockSpec(a.shape, lambda i: (0, 0))
    layer3 = lambda a: pl.BlockSpec((None,) + a.shape[1:], lambda i: (layer, 0, 0))
    row = lambda w: pl.BlockSpec((tm, w), lambda i: (i, 0))
    return pl.pallas_call(
        functools.partial(_merge_kernel, alpha=alpha),
        grid=(n_rows // tm,),
        in_specs=[row(A_Q), row(A_Q), row(d), const2(og), layer3(w_out_b), modspec(2),
                  const2(lg), const2(lb), modspec(3), modspec(4), layer3(wr_p), const2(br_p)],
        out_specs=[row(d), pl.BlockSpec((tm * chunks, LANES), lambda i: (i, 0)),
                   row(LANES), row(LANES), pl.BlockSpec((1, LANES), lambda i: (0, 0))],
        out_shape=[jax.ShapeDtypeStruct((n_rows, d), F32),
                   jax.ShapeDtypeStruct((n_rows * chunks, LANES), F32),
                   jax.ShapeDtypeStruct((n_rows, LANES), jnp.int32),
                   jax.ShapeDtypeStruct((n_rows, LANES), F32),
                   jax.ShapeDtypeStruct((1, LANES), F32)],
        compiler_params=_cparams(("arbitrary",)),
        name="merge_route",
    )(o_a, o_b, x_all, og, w_out_b, modl, lg, lb, modl, modl, wr_p, br_p)


def _tile_meta(cnt, n_flat, tm):
    i32 = jnp.int32
    counts = cnt[0, :N_EXPERTS].astype(i32)
    n_tiles = n_flat // tm + N_EXPERTS
    tiles_per = (counts + tm - 1) // tm
    tile_end = jnp.cumsum(tiles_per).astype(i32)
    tile_start = tile_end - tiles_per
    tid = jnp.arange(n_tiles, dtype=i32)
    used = tid < tile_end[-1]
    te = jnp.minimum(jnp.sum((tile_end[None, :] <= tid[:, None]).astype(i32), axis=1), N_EXPERTS - 1)
    onehot = te[:, None] == jnp.arange(N_EXPERTS, dtype=i32)[None, :]
    cnt_t = jnp.sum(jnp.where(onehot, counts[None, :], 0), axis=1)
    start_t = jnp.sum(jnp.where(onehot, tile_start[None, :], 0), axis=1)
    tvalid = jnp.where(used, jnp.clip(cnt_t - (tid - start_t) * tm, 0, tm), 0).astype(i32)
    tfirst = (used & (tid == start_t)).astype(i32)
    txb = jnp.where(used, tid, tile_end[-1] - 1).astype(i32)
    tzero = (tvalid < tm).astype(i32)
    return (tile_start * tm).astype(i32), te.astype(i32), tfirst, tvalid, txb, tzero


def _dispatch_kernel(eid_ref, rank_ref, pstart_ref, tzero_ref, h2_ref, xs_hbm, zbuf, sem, zsem,
                     *, tb, tm, chunks, n_tiles):
    i = pl.program_id(0)

    @pl.when(i == 0)
    def _():
        zbuf[...] = jnp.zeros(zbuf.shape, F32)

        def fill(t):
            off = pl.multiple_of(t * (tm * chunks), tm * chunks)
            return pltpu.make_async_copy(zbuf, xs_hbm.at[pl.ds(off, tm * chunks), :], zsem)

        def start(t, c):
            @pl.when(tzero_ref[t] > 0)
            def _():
                fill(t).start()
            return c

        def wait(t, c):
            @pl.when(tzero_ref[t] > 0)
            def _():
                fill(t).wait()
            return c

        lax.fori_loop(0, n_tiles, start, 0)
        lax.fori_loop(0, n_tiles, wait, 0)

    def start_row(j, c):
        flat = i * (tb * TOP_K) + j
        s = pl.multiple_of((j // TOP_K) * chunks, chunks)
        t = pl.multiple_of((pstart_ref[eid_ref[flat]] + rank_ref[flat]) * chunks, chunks)
        pltpu.make_async_copy(h2_ref.at[pl.ds(s, chunks), :],
                              xs_hbm.at[pl.ds(t, chunks), :], sem).start()
        return c

    lax.fori_loop(0, tb * TOP_K, start_row, 0, unroll=8)
    for _ in range(TOP_K):
        pltpu.make_async_copy(h2_ref, xs_hbm.at[pl.ds(0, tb * chunks), :], sem).wait()


def _dispatch(eid, rank, pstart, tzero, h2_lin, n_tok, d, tm, tb=256):
    chunks = d // LANES
    n_tiles = tzero.shape[0]
    return pl.pallas_call(
        functools.partial(_dispatch_kernel, tb=tb, tm=tm, chunks=chunks, n_tiles=n_tiles),
        grid_spec=pltpu.PrefetchScalarGridSpec(
            num_scalar_prefetch=4,
            grid=(n_tok // tb,),
            in_specs=[pl.BlockSpec((tb * chunks, LANES), lambda i, *_: (i, 0))],
            out_specs=pl.BlockSpec(memory_space=pl.ANY),
            scratch_shapes=[pltpu.VMEM((tm * chunks, LANES), F32),
                            pltpu.SemaphoreType.DMA, pltpu.SemaphoreType.DMA],
        ),
        out_shape=jax.ShapeDtypeStruct((n_tiles * tm * chunks, LANES), F32),
        compiler_params=_cparams(("arbitrary",)),
        name="dispatch",
    )(eid, rank, pstart, tzero, h2_lin)


def _expert_kernel(te_ref, tfirst_ref, tvalid_ref, txb_ref,
                   x_ref, w1_ref, b1_ref, w2_ref, b2_ref, y_ref, w2b, *, tm, chunks):
    i = pl.program_id(0)

    @pl.when(tvalid_ref[i] > 0)
    def _():
        @pl.when(tfirst_ref[i] > 0)
        def _():
            w2b[...] = w2_ref[...].astype(BF16)

        x = jnp.concatenate([x_ref[pl.ds(c, tm, stride=chunks), :] for c in range(chunks)],
                            axis=1).astype(BF16)
        u = jnp.dot(x, w1_ref[...], preferred_element_type=F32) + b1_ref[...]
        glu = jnp.minimum(u[:, :D_EXPERT], SWIGLU_LIMIT)
        lin = jnp.clip(u[:, D_EXPERT:], -SWIGLU_LIMIT, SWIGLU_LIMIT)
        a = glu * jax.nn.sigmoid(SWIGLU_ALPHA * glu) * (lin + 1.0)
        y = jnp.dot(a.astype(BF16), w2b[...], preferred_element_type=F32) + b2_ref[...]
        for c in range(chunks):
            y_ref[pl.ds(c, tm, stride=chunks), :] = y[:, c * LANES:(c + 1) * LANES]

    @pl.when(tvalid_ref[i] == 0)
    def _():
        y_ref[...] = jnp.zeros(y_ref.shape, F32)


def _experts(te, tfirst, tvalid, txb, xs_lin, w1p, b1p, w2, b2, layer, d, tm):
    n_tiles = te.shape[0]
    chunks = d // LANES
    e0 = layer * N_EXPERTS
    return pl.pallas_call(
        functools.partial(_expert_kernel, tm=tm, chunks=chunks),
        grid_spec=pltpu.PrefetchScalarGridSpec(
            num_scalar_prefetch=4,
            grid=(n_tiles,),
            in_specs=[
                pl.BlockSpec((tm * chunks, LANES), lambda i, te, tf, tv, txb: (txb[i], 0)),
                pl.BlockSpec((None, d, 2 * D_EXPERT), lambda i, te, *_: (e0 + te[i], 0, 0)),
                pl.BlockSpec((None, 1, 2 * D_EXPERT), lambda i, te, *_: (e0 + te[i], 0, 0)),
                pl.BlockSpec((None, D_EXPERT, d), lambda i, te, *_: (e0 + te[i], 0, 0)),
                pl.BlockSpec((None, 1, d), lambda i, te, *_: (e0 + te[i], 0, 0)),
            ],
            out_specs=pl.BlockSpec((tm * chunks, LANES), lambda i, *_: (i, 0)),
            scratch_shapes=[pltpu.VMEM((D_EXPERT, d), BF16)],
        ),
        out_shape=jax.ShapeDtypeStruct((n_tiles * tm * chunks, LANES), F32),
        compiler_params=_cparams(("arbitrary",)),
        name="experts",
    )(te, tfirst, tvalid, txb, xs_lin, w1p, b1p, w2, b2)


def _combine_kernel(eid_ref, rank_ref, pstart_ref, ys_hbm, tw_ref, x_ref, g2_ref, lg_ref, lb_ref,
                    o_ref, gbuf, sem, *, alpha, tm, chunks):
    i = pl.program_id(0)
    n = pl.num_programs(0)
    slot = i % 2

    def row(step, slot_, j):
        flat = step * (tm * TOP_K) + j
        s = pl.multiple_of((pstart_ref[eid_ref[flat]] + rank_ref[flat]) * chunks, chunks)
        r = pl.multiple_of((j // TOP_K) * chunks, chunks)
        return pltpu.make_async_copy(ys_hbm.at[pl.ds(s, chunks), :],
                                     gbuf.at[slot_, j % TOP_K, pl.ds(r, chunks), :], sem.at[slot_])

    def issue(step, slot_):
        def body(j, c):
            row(step, slot_, j).start()
            return c
        lax.fori_loop(0, tm * TOP_K, body, 0, unroll=8)

    @pl.when(i == 0)
    def _():
        issue(0, 0)

    @pl.when(i + 1 < n)
    def _():
        issue(i + 1, 1 - slot)

    for k in range(TOP_K):
        pltpu.make_async_copy(ys_hbm.at[pl.ds(0, tm * chunks), :], gbuf.at[slot, k],
                              sem.at[slot]).wait()

    tw = tw_ref[...]
    cols = []
    for c in range(chunks):
        sl = pl.ds(c, tm, stride=chunks)
        acc = tw[:, 0:1] * gbuf[slot, 0, sl, :]
        for k in range(1, TOP_K):
            acc += tw[:, k:k + 1] * gbuf[slot, k, sl, :]
        cols.append(acc)
    y = jnp.concatenate(cols, axis=1)
    o_ref[...] = _ln(alpha * x_ref[...] + g2_ref[...] * y) * lg_ref[...] + lb_ref[...]


def _combine(eid, rank, pstart, ys_lin, top_w, x1, modl, lg, lb, n_rows, n_lat_rows, tm, alpha):
    d = x1.shape[1]
    chunks = d // LANES
    n_lat_tiles = n_lat_rows // tm
    which = lambda i: jnp.where(i >= n_lat_tiles, 1, 0)
    const2 = lambda a: pl.BlockSpec(a.shape, lambda i, *_: (0, 0))
    return pl.pallas_call(
        functools.partial(_combine_kernel, alpha=alpha, tm=tm, chunks=chunks),
        grid_spec=pltpu.PrefetchScalarGridSpec(
            num_scalar_prefetch=3,
            grid=(n_rows // tm,),
            in_specs=[pl.BlockSpec(memory_space=pl.ANY),
                      pl.BlockSpec((tm, LANES), lambda i, *_: (i, 0)),
                      pl.BlockSpec((tm, d), lambda i, *_: (i, 0)),
                      pl.BlockSpec((None, 1, d), lambda i, *_: (which(i) * 6 + 5, 0, 0)),
                      const2(lg), const2(lb)],
            out_specs=pl.BlockSpec((tm, d), lambda i, *_: (i, 0)),
            scratch_shapes=[pltpu.VMEM((2, TOP_K, tm * chunks, LANES), F32),
                            pltpu.SemaphoreType.DMA((2,))],
        ),
        out_shape=jax.ShapeDtypeStruct((n_rows, d), F32),
        compiler_params=_cparams(("arbitrary",)),
        name="combine",
    )(eid, rank, pstart, ys_lin, top_w, x1, modl, lg, lb)


def _rope_tables(s_len, n_ctx):
    rows = s_len // GRID_W
    axis_dim = HEAD_DIM // 2
    row = jnp.repeat(jnp.arange(rows, dtype=F32), GRID_W)
    col = jnp.tile(jnp.arange(GRID_W, dtype=F32), rows)
    inv = ROPE_THETA ** (-jnp.arange(0, axis_dim, 2, dtype=F32) / axis_dim)
    ar = row[:, None] * inv[None, :]
    ac = col[:, None] * inv[None, :]
    ang = jnp.concatenate([ar, ar, ac, ac], -1)
    cos = jnp.tile(jnp.cos(ang), (1, 2))
    sin = jnp.tile(jnp.sin(ang), (1, 2))
    lo = (jnp.arange(LANES) % (axis_dim) < axis_dim // 2)[None, :]
    sa = jnp.where(lo, -sin, 0.0)
    sb = jnp.where(lo, 0.0, sin)
    pad = lambda t, v: jnp.concatenate([t, jnp.full((n_ctx, LANES), v, F32)], 0)
    return pad(cos, 1.0), pad(sa, 0.0), pad(sb, 0.0)


def kernel(x, c, ctx, c_ctx, w_ada, b_ada, w_in, q_gain, k_gain, sink, out_gain, w_out,
           ln1_g, ln1_b, w_router, b_router, w1, b1, w2, b2, ln2_g, ln2_b):
    bsz, s_len, d = x.shape
    n_ctx = ctx.shape[1]
    depth = w_ada.shape[0]
    assert bsz == 1 and d % LANES == 0
    nt = s_len + n_ctx
    alpha = (2.0 * depth) ** 0.25
    tm = 256

    condT = jnp.stack([c[0], c_ctx], axis=1)
    mod = _modulation(condT, w_ada, b_ada).reshape(depth, 12, 1, d)
    cos_t, sa_t, sb_t = _rope_tables(s_len, n_ctx)
    hid = jnp.arange(A_Q) // HEAD_DIM
    bd = (hid[:, None] == hid[None, :]).astype(BF16)
    w_in_b = w_in.astype(BF16)
    w_out_b = w_out.astype(BF16)
    n_e = w1.shape[1]
    w1p = _w1_prep(w1)
    b1p = jnp.concatenate([b1[..., 0::2], b1[..., 1::2]], axis=-1).reshape(depth * n_e, 1, -1)
    w2r = w2.reshape(depth * n_e, D_EXPERT, d)
    b2r = b2.reshape(depth * n_e, 1, d)
    wr_p = jnp.pad(w_router, ((0, 0), (0, 0), (0, LANES - N_EXPERTS)))
    br_p = jnp.pad(b_router, ((0, 0), (0, LANES - N_EXPERTS)), constant_values=NEG_BIG)

    x_all = jnp.concatenate([x[0], ctx[0]], axis=0)
    for l in range(depth):
        last = l == depth - 1
        n_rows = s_len if last else nt
        modl = mod[l]
        gq_t = jnp.tile(q_gain[l], N_HEADS)[None, :]
        gk_t = jnp.tile(k_gain[l], KV_HEADS)[None, :]
        qa, kaT, va, qb, kbT, vb = _project(x_all, modl, w_in_b, l, gq_t, gk_t, bd,
                                            cos_t, sa_t, sb_t, s_len, tm)
        o_a = _attn_a(sink[l], qa, kaT, va, n_rows, s_len)
        o_b = _attn_b(qb, kbT, vb, n_rows, s_len)
        x1, h2_lin, top_i, top_w, cnt = _merge(
            o_a, o_b, x_all, modl, out_gain[l][None, :], w_out_b, l, ln1_g[l][None, :],
            ln1_b[l][None, :], wr_p, br_p[l][None, :], n_rows, s_len, tm, alpha)
        pstart, te, tfirst, tvalid, txb, tzero = _tile_meta(cnt, n_rows * TOP_K, tm)
        eid = top_i[:, :TOP_K].reshape(-1)
        rank = top_i[:, TOP_K:2 * TOP_K].reshape(-1)
        xs_lin = _dispatch(eid, rank, pstart, tzero, h2_lin, n_rows, d, tm)
        ys_lin = _experts(te, tfirst, tvalid, txb, xs_lin, w1p, b1p, w2r, b2r, l, d, tm)
        x_all = _combine(eid, rank, pstart, ys_lin, top_w, x1, modl, ln2_g[l][None, :],
                         ln2_b[l][None, :], n_rows, s_len, tm, alpha)
    return x_all[None]
```

```python
import functools

import jax
import jax.numpy as jnp
from jax import lax
from jax.experimental import pallas as pl
from jax.experimental.pallas import tpu as pltpu

F32 = jnp.float32
BF16 = jnp.bfloat16

GRID_W = 64
HEAD_DIM = 64
N_HEADS = 16
KV_HEADS = 2
GROUP = N_HEADS // KV_HEADS
WINDOW = 128
BLOCK = 128
ROPE_THETA = 10000.0
ATTN_SCALE = HEAD_DIM ** -0.5
A_Q = N_HEADS * HEAD_DIM
A_KV = KV_HEADS * HEAD_DIM
N_EXPERTS = 32
TOP_K = 4
D_EXPERT = 512
SWIGLU_LIMIT = 7.0
SWIGLU_ALPHA = 1.702
LN_EPS = 1e-5
RMS_EPS = 1e-6
LOG2_E = 1.4426950408889634
LANES = 128
NEG_BIG = -1e30

VMEM_LIMIT = 56 * 1024 * 1024


def _cparams(sem):
    return pltpu.CompilerParams(dimension_semantics=sem, vmem_limit_bytes=VMEM_LIMIT)


def _ln(x):
    mu = jnp.mean(x, -1, keepdims=True)
    xc = x - mu
    var = jnp.mean(xc * xc, -1, keepdims=True)
    return xc * lax.rsqrt(var + LN_EPS)


def _mod_kernel(condT_ref, w_ref, b_ref, o_ref):
    ct = condT_ref[...]
    s = ct * jax.nn.sigmoid(ct)
    w = w_ref[...]
    r0 = jnp.sum(s[:, 0:1] * w, axis=0, keepdims=True)
    r1 = jnp.sum(s[:, 1:2] * w, axis=0, keepdims=True)
    o_ref[...] = jnp.concatenate([r0, r1], axis=0) + b_ref[...]


def _modulation(condT, w_ada, b_ada):
    depth, d, e = w_ada.shape
    tn = 512
    return pl.pallas_call(
        _mod_kernel,
        grid=(depth, e // tn),
        in_specs=[
            pl.BlockSpec((d, 2), lambda l, j: (0, 0)),
            pl.BlockSpec((None, d, tn), lambda l, j: (l, 0, j)),
            pl.BlockSpec((None, 1, tn), lambda l, j: (l, 0, j)),
        ],
        out_specs=pl.BlockSpec((None, 2, tn), lambda l, j: (l, 0, j)),
        out_shape=jax.ShapeDtypeStruct((depth, 2, e), F32),
        compiler_params=_cparams(("parallel", "parallel")),
        name="modulation",
    )(condT, w_ada, b_ada.reshape(depth, 1, e))


def _proj_kernel(x_ref, sh_ref, sc_ref, w_ref, gq_ref, gk_ref, bd_ref,
                 cos_ref, sa_ref, sb_ref,
                 qa_ref, kaT_ref, va_ref, qb_ref, kbT_ref, vb_ref):
    x = x_ref[...]
    h = _ln(x) * (1.0 + sc_ref[...]) + sh_ref[...]
    p = jnp.dot(h.astype(BF16), w_ref[...], preferred_element_type=F32)
    cos, sa, sb = cos_ref[...], sa_ref[...], sb_ref[...]
    lane = lax.broadcasted_iota(jnp.int32, cos.shape, 1)

    def rope(t):
        return (t * cos + pltpu.roll(t, LANES - 16, 1) * sa
                + pltpu.roll(t, 16, 1) * sb)

    def head_rms(t, width):
        sq = t * t
        hi = sq.astype(BF16)
        lo = (sq - hi.astype(F32)).astype(BF16)
        bd = bd_ref[0:width, 0:width]
        ss = (jnp.dot(hi, bd, preferred_element_type=F32)
              + jnp.dot(lo, bd, preferred_element_type=F32))
        return lax.rsqrt(ss * (1.0 / HEAD_DIM) + RMS_EPS)

    def with_ones(t, g):
        src = t if g == 0 else pltpu.roll(t, HEAD_DIM, 1)
        return jnp.where(lane < HEAD_DIM, src,
                         jnp.where(lane == HEAD_DIM, 1.0, 0.0)).astype(BF16)

    def write_q(q_out, q, scale):
        for c in range(A_Q // LANES):
            r = rope(q[:, c * LANES:(c + 1) * LANES]) * scale
            q_out[2 * c] = r[:, :HEAD_DIM].astype(BF16)
            q_out[2 * c + 1] = r[:, HEAD_DIM:].astype(BF16)

    o = 0
    write_q(qa_ref, p[:, o:o + A_Q], ATTN_SCALE); o += A_Q
    kaT_ref[...] = rope(p[:, o:o + A_KV]).T.astype(BF16); o += A_KV
    va = p[:, o:o + A_KV]; o += A_KV
    va_ref[0] = with_ones(va, 0)
    va_ref[1] = with_ones(va, 1)
    qb = p[:, o:o + A_Q]; o += A_Q
    write_q(qb_ref, qb * head_rms(qb, A_Q) * gq_ref[...], ATTN_SCALE * LOG2_E)
    kb = p[:, o:o + A_KV]; o += A_KV
    kbT_ref[...] = rope(kb * head_rms(kb, A_KV) * gk_ref[...]).T.astype(BF16)
    vb = p[:, o:o + A_KV]
    vb_ref[0] = with_ones(vb, 0)
    vb_ref[1] = with_ones(vb, 1)


def _project(x_all, modl, w_in_b, layer, gq_t, gk_t, bd, cos_t, sa_t, sb_t, n_lat_rows, tm):
    nt, d = x_all.shape
    n_lat_tiles = n_lat_rows // tm
    which = lambda i: jnp.where(i >= n_lat_tiles, 1, 0)
    modspec = lambda j: pl.BlockSpec((None, 1, d), lambda i: (which(i) * 6 + j, 0, 0))
    const2 = lambda a: pl.BlockSpec(a.shape, lambda i: (0, 0))
    layer3 = lambda a: pl.BlockSpec((None,) + a.shape[1:], lambda i: (layer, 0, 0))
    q_shape = jax.ShapeDtypeStruct((N_HEADS, nt, HEAD_DIM), BF16)
    kT_shape = jax.ShapeDtypeStruct((A_KV, nt), BF16)
    v_shape = jax.ShapeDtypeStruct((KV_HEADS, nt, LANES), BF16)
    q_spec = pl.BlockSpec((N_HEADS, tm, HEAD_DIM), lambda i: (0, i, 0))
    kT_spec = pl.BlockSpec((A_KV, tm), lambda i: (0, i))
    v_spec = pl.BlockSpec((KV_HEADS, tm, LANES), lambda i: (0, i, 0))
    tab_spec = pl.BlockSpec((tm, LANES), lambda i: (i, 0))
    return pl.pallas_call(
        _proj_kernel,
        grid=(nt // tm,),
        in_specs=[pl.BlockSpec((tm, d), lambda i: (i, 0)), modspec(0), modspec(1),
                  layer3(w_in_b), const2(gq_t), const2(gk_t), const2(bd),
                  tab_spec, tab_spec, tab_spec],
        out_specs=[q_spec, kT_spec, v_spec, q_spec, kT_spec, v_spec],
        out_shape=[q_shape, kT_shape, v_shape, q_shape, kT_shape, v_shape],
        compiler_params=_cparams(("parallel",)),
        name="project",
    )(x_all, modl, modl, w_in_b, gq_t, gk_t, bd, cos_t, sa_t, sb_t)


def _attn_b_kernel(q_ref, kT_ref, v_ref, o_ref, m_scr, acc_scr, *, n_lat_rows, tk, blocks_per_step):
    i = pl.program_id(0)
    tq = q_ref.shape[1]
    nt = kT_ref.shape[1]
    is_ctx = i * tq >= n_lat_rows
    m_scr[...] = jnp.full(m_scr.shape, NEG_BIG, F32)
    acc_scr[...] = jnp.zeros(acc_scr.shape, F32)

    def keys(off, width):
        for h in range(N_HEADS):
            g = h // GROUP
            kt = kT_ref[g * HEAD_DIM:(g + 1) * HEAD_DIM, pl.ds(off, width)]
            s = jnp.dot(q_ref[h], kt, preferred_element_type=F32)
            m_prev = m_scr[h]
            m_new = jnp.maximum(m_prev, jnp.max(s, axis=1, keepdims=True))
            alpha = jnp.exp2(m_prev - m_new)
            p = jnp.exp2(s - jnp.concatenate([m_new] * (width // LANES), axis=1))
            pv = jnp.dot(p.astype(BF16), v_ref[g, pl.ds(off, width), :],
                         preferred_element_type=F32)
            acc_scr[h] = acc_scr[h] * alpha + pv
            m_scr[h] = m_new

    def step(j, carry):
        for u in range(blocks_per_step):
            keys(pl.multiple_of(j * (tk * blocks_per_step) + u * tk, tk), tk)
        return carry

    lax.fori_loop(0, jnp.where(is_ctx, 0, n_lat_rows // (tk * blocks_per_step)), step, 0)
    keys(n_lat_rows, nt - n_lat_rows)
    for h in range(N_HEADS):
        acc = acc_scr[h]
        o_ref[:, h * HEAD_DIM:(h + 1) * HEAD_DIM] = (
            acc[:, :HEAD_DIM] / acc[:, HEAD_DIM:HEAD_DIM + 1])


def _attn_b(q_hm, kT, v2, n_q_rows, n_lat_rows, tq=128, tk=256, blocks_per_step=2):
    nt = kT.shape[1]
    assert n_lat_rows % (tk * blocks_per_step) == 0
    return pl.pallas_call(
        functools.partial(_attn_b_kernel, n_lat_rows=n_lat_rows, tk=tk,
                          blocks_per_step=blocks_per_step),
        grid=(n_q_rows // tq,),
        in_specs=[pl.BlockSpec((N_HEADS, tq, HEAD_DIM), lambda i: (0, i, 0)),
                  pl.BlockSpec((A_KV, nt), lambda i: (0, 0)),
                  pl.BlockSpec((KV_HEADS, nt, LANES), lambda i: (0, 0, 0))],
        out_specs=pl.BlockSpec((tq, A_Q), lambda i: (i, 0)),
        out_shape=jax.ShapeDtypeStruct((n_q_rows, A_Q), F32),
        scratch_shapes=[pltpu.VMEM((N_HEADS, tq, LANES), F32),
                        pltpu.VMEM((N_HEADS, tq, LANES), F32)],
        compiler_params=_cparams(("parallel",)),
        name="attn_global",
    )(q_hm, kT, v2)


def _attn_a_kernel(sink_ref, q_ref, kT_ref, v_ref, o_ref, *, n_lat_rows):
    n = pl.program_id(0)
    n_lat_blocks = n_lat_rows // BLOCK
    nt = kT_ref.shape[1]
    is_lat = n < n_lat_blocks
    ii = lax.broadcasted_iota(jnp.int32, (BLOCK, BLOCK), 0)
    jj = lax.broadcasted_iota(jnp.int32, (BLOCK, BLOCK), 1)
    blocks = []
    for d in (-1, 0, 1):
        blk = n + d
        valid = is_lat & (blk >= 0) & (blk < n_lat_blocks)
        off = pl.multiple_of(jnp.clip(blk, 0, n_lat_blocks - 1) * BLOCK, BLOCK)
        rel_ok = (jj >= ii) if d == -1 else ((jj <= ii) if d == 1 else (jj >= 0))
        blocks.append((off, valid & rel_ok))

    for h in range(N_HEADS):
        g = h // GROUP
        q = q_ref[h]
        kg = kT_ref.at[g * HEAD_DIM:(g + 1) * HEAD_DIM, :]
        vg = v_ref.at[g]
        sink = sink_ref[h]
        s_list = []
        for off, mask in blocks:
            s = jnp.dot(q, kg[:, pl.ds(off, BLOCK)], preferred_element_type=F32)
            s_list.append(jnp.where(mask, s, NEG_BIG))
        s_ctx = jnp.dot(q, kg[:, n_lat_rows:nt], preferred_element_type=F32)
        col_max = s_list[0]
        for s in s_list[1:]:
            col_max = jnp.maximum(col_max, s)
        for c in range((nt - n_lat_rows) // BLOCK):
            col_max = jnp.maximum(col_max, s_ctx[:, c * BLOCK:(c + 1) * BLOCK])
        m = jnp.maximum(jnp.max(col_max, axis=1, keepdims=True), sink)
        acc = jnp.dot(jnp.exp(s_ctx - m).astype(BF16), vg[n_lat_rows:nt, :],
                      preferred_element_type=F32)
        for (off, _), s in zip(blocks, s_list):
            acc += jnp.dot(jnp.exp(s - m).astype(BF16), vg[pl.ds(off, BLOCK), :],
                           preferred_element_type=F32)
        denom = acc[:, HEAD_DIM:HEAD_DIM + 1] + jnp.exp(sink - m)
        o_ref[:, h * HEAD_DIM:(h + 1) * HEAD_DIM] = acc[:, :HEAD_DIM] / denom


def _attn_a(sink_l, q_hm, kT, v2, n_q_rows, n_lat_rows):
    nt = kT.shape[1]
    return pl.pallas_call(
        functools.partial(_attn_a_kernel, n_lat_rows=n_lat_rows),
        grid_spec=pltpu.PrefetchScalarGridSpec(
            num_scalar_prefetch=1,
            grid=(n_q_rows // BLOCK,),
            in_specs=[pl.BlockSpec((N_HEADS, BLOCK, HEAD_DIM), lambda i, s: (0, i, 0)),
                      pl.BlockSpec((A_KV, nt), lambda i, s: (0, 0)),
                      pl.BlockSpec((KV_HEADS, nt, LANES), lambda i, s: (0, 0, 0))],
            out_specs=pl.BlockSpec((BLOCK, A_Q), lambda i, s: (i, 0)),
        ),
        out_shape=jax.ShapeDtypeStruct((n_q_rows, A_Q), F32),
        compiler_params=_cparams(("parallel",)),
        name="attn_window",
    )(sink_l, q_hm, kT, v2)


def _w1_prep_kernel(w_ref, se_ref, so_ref, o_ref):
    wb = w_ref[...].astype(BF16)
    half = D_EXPERT // 2
    for b in range(2):
        blk = wb[:, 2 * half * b:2 * half * (b + 1)]
        o_ref[:, half * b:half * (b + 1)] = jnp.dot(
            blk, se_ref[...], preferred_element_type=F32).astype(BF16)
        o_ref[:, D_EXPERT + half * b:D_EXPERT + half * (b + 1)] = jnp.dot(
            blk, so_ref[...], preferred_element_type=F32).astype(BF16)


def _w1_prep(w1):
    depth, n_e, d, n2 = w1.shape
    j = jnp.arange(D_EXPERT)[:, None]
    m = jnp.arange(D_EXPERT // 2)[None, :]
    sel_even = (j == 2 * m).astype(BF16)
    sel_odd = (j == 2 * m + 1).astype(BF16)
    const2 = lambda a: pl.BlockSpec(a.shape, lambda i: (0, 0))
    out = pl.pallas_call(
        _w1_prep_kernel,
        grid=(depth * n_e,),
        in_specs=[pl.BlockSpec((None, d, n2), lambda i: (i, 0, 0)), const2(sel_even), const2(sel_odd)],
        out_specs=pl.BlockSpec((None, d, n2), lambda i: (i, 0, 0)),
        out_shape=jax.ShapeDtypeStruct((depth * n_e, d, n2), BF16),
        compiler_params=_cparams(("parallel",)),
        name="w1_prep",
    )(w1.reshape(depth * n_e, d, n2), sel_even, sel_odd)
    return out


def _merge_kernel(oa_ref, ob_ref, x_ref, og_ref, wout_ref, g1_ref, lg_ref, lb_ref,
                  sh2_ref, sc2_ref, wrh_ref, wrl_ref, br_ref,
                  x1_ref, h2_ref, ti_ref, tw_ref, cnt_ref, *, alpha):
    def rms(t, g):
        return t * lax.rsqrt(jnp.mean(t * t, -1, keepdims=True) + RMS_EPS) * g

    og = og_ref[...]
    o = jnp.concatenate([rms(oa_ref[...], og[:, :A_Q]), rms(ob_ref[...], og[:, A_Q:])], axis=1)
    mix = jnp.dot(o.astype(BF16), wout_ref[...], preferred_element_type=F32)
    x1 = _ln(alpha * x_ref[...] + g1_ref[...] * mix) * lg_ref[...] + lb_ref[...]
    x1_ref[...] = x1
    h2 = _ln(x1) * (1.0 + sc2_ref[...]) + sh2_ref[...]
    tm, d = h2.shape
    chunks = d // LANES
    for c in range(chunks):
        h2_ref[pl.ds(c, tm, stride=chunks), :] = h2[:, c * LANES:(c + 1) * LANES]

    h_hi = h2.astype(BF16)
    h_lo = (h2 - h_hi.astype(F32)).astype(BF16)
    w_hi = wrh_ref[...]
    logits = ((jnp.dot(h_hi, w_hi, preferred_element_type=F32)
               + jnp.dot(h_lo, w_hi, preferred_element_type=F32))
              + jnp.dot(h_hi, wrl_ref[...], preferred_element_type=F32)) + br_ref[...]
    lane = lax.broadcasted_iota(jnp.int32, logits.shape, 1)
    lane_f = lane.astype(F32)
    work = logits
    vals, ids = [], []
    for _ in range(TOP_K):
        mx = jnp.max(work, axis=1, keepdims=True)
        idx = jnp.min(jnp.where(work == mx, lane_f, float(LANES)), axis=1, keepdims=True)
        vals.append(mx)
        ids.append(idx)
        work = jnp.where(lane_f == idx, NEG_BIG, work)
    es = [jnp.exp(v - vals[0]) for v in vals]
    tot = es[0] + es[1] + es[2] + es[3]

    @pl.when(pl.program_id(0) == 0)
    def _():
        cnt_ref[...] = jnp.zeros(cnt_ref.shape, F32)

    onehot = [(lane_f == ids[k]).astype(F32) for k in range(TOP_K)]
    picked = (onehot[0] + onehot[1]) + (onehot[2] + onehot[3])
    earlier = (lax.broadcasted_iota(jnp.int32, (tm, tm), 1)
               < lax.broadcasted_iota(jnp.int32, (tm, tm), 0)).astype(BF16)
    before = jnp.dot(earlier, picked.astype(BF16), preferred_element_type=F32) + cnt_ref[...]
    ranks = [jnp.sum(onehot[k] * before, axis=1, keepdims=True) for k in range(TOP_K)]
    cnt_ref[...] += jnp.sum(picked, axis=0, keepdims=True)

    ti = jnp.zeros(logits.shape, jnp.int32)
    tw = jnp.zeros(logits.shape, F32)
    for k in range(TOP_K):
        ti = jnp.where(lane == k, ids[k].astype(jnp.int32), ti)
        ti = jnp.where(lane == TOP_K + k, ranks[k].astype(jnp.int32), ti)
        tw = jnp.where(lane == k, es[k] / tot, tw)
    ti_ref[...] = ti
    tw_ref[...] = tw


def _merge(o_a, o_b, x_all, modl, og, w_out_b, layer, lg, lb, wr_hi, wr_lo, br_p, n_rows,
           n_lat_rows, tm, alpha):
    d = x_all.shape[1]
    chunks = d // LANES
    n_lat_tiles = n_lat_rows // tm
    which = lambda i: jnp.where(i >= n_lat_tiles, 1, 0)
    modspec = lambda j: pl.BlockSpec((None, 1, d), lambda i: (which(i) * 6 + j, 0, 0))
    const2 = lambda a: pl.BlockSpec(a.shape, lambda i: (0, 0))
    layer3 = lambda a: pl.BlockSpec((None,) + a.shape[1:], lambda i: (layer, 0, 0))
    row = lambda w: pl.BlockSpec((tm, w), lambda i: (i, 0))
    return pl.pallas_call(
        functools.partial(_merge_kernel, alpha=alpha),
        grid=(n_rows // tm,),
        in_specs=[row(A_Q), row(A_Q), row(d), const2(og), layer3(w_out_b), modspec(2),
                  const2(lg), const2(lb), modspec(3), modspec(4), layer3(wr_hi), layer3(wr_lo),
                  const2(br_p)],
        out_specs=[row(d), pl.BlockSpec((tm * chunks, LANES), lambda i: (i, 0)),
                   row(LANES), row(LANES), pl.BlockSpec((1, LANES), lambda i: (0, 0))],
        out_shape=[jax.ShapeDtypeStruct((n_rows, d), F32),
                   jax.ShapeDtypeStruct((n_rows * chunks, LANES), F32),
                   jax.ShapeDtypeStruct((n_rows, LANES), jnp.int32),
                   jax.ShapeDtypeStruct((n_rows, LANES), F32),
                   jax.ShapeDtypeStruct((1, LANES), F32)],
        compiler_params=_cparams(("arbitrary",)),
        name="merge_route",
    )(o_a, o_b, x_all, og, w_out_b, modl, lg, lb, modl, modl, wr_hi, wr_lo, br_p)


def _tile_meta(cnt, n_flat, tm):
    i32 = jnp.int32
    counts = cnt[0, :N_EXPERTS].astype(i32)
    n_tiles = n_flat // tm + N_EXPERTS
    tiles_per = (counts + tm - 1) // tm
    tile_end = jnp.cumsum(tiles_per).astype(i32)
    tile_start = tile_end - tiles_per
    tid = jnp.arange(n_tiles, dtype=i32)
    used = tid < tile_end[-1]
    te = jnp.minimum(jnp.sum((tile_end[None, :] <= tid[:, None]).astype(i32), axis=1), N_EXPERTS - 1)
    onehot = te[:, None] == jnp.arange(N_EXPERTS, dtype=i32)[None, :]
    cnt_t = jnp.sum(jnp.where(onehot, counts[None, :], 0), axis=1)
    start_t = jnp.sum(jnp.where(onehot, tile_start[None, :], 0), axis=1)
    tvalid = jnp.where(used, jnp.clip(cnt_t - (tid - start_t) * tm, 0, tm), 0).astype(i32)
    tfirst = (used & (tid == start_t)).astype(i32)
    txb = jnp.where(used, tid, tile_end[-1] - 1).astype(i32)
    tzero = (tvalid < tm).astype(i32)
    return (tile_start * tm).astype(i32), te.astype(i32), tfirst, tvalid, txb, tzero


def _dispatch_kernel(pos_ref, tzero_ref, h2_ref, xs_hbm, zbuf, sem, zsem,
                     *, tb, tm, chunks, n_tiles):
    i = pl.program_id(0)

    @pl.when(i == 0)
    def _():
        zbuf[...] = jnp.zeros(zbuf.shape, F32)

        def fill(t):
            off = pl.multiple_of(t * (tm * chunks), tm * chunks)
            return pltpu.make_async_copy(zbuf, xs_hbm.at[pl.ds(off, tm * chunks), :], zsem)

        def start(t, c):
            @pl.when(tzero_ref[t] > 0)
            def _():
                fill(t).start()
            return c

        def wait(t, c):
            @pl.when(tzero_ref[t] > 0)
            def _():
                fill(t).wait()
            return c

        lax.fori_loop(0, n_tiles, start, 0)
        lax.fori_loop(0, n_tiles, wait, 0)

    def start_token(r, c):
        src = h2_ref.at[pl.ds(pl.multiple_of(r * chunks, chunks), chunks), :]
        base = (i * tb + r) * TOP_K
        for k in range(TOP_K):
            t = pl.multiple_of(pos_ref[base + k] * chunks, chunks)
            pltpu.make_async_copy(src, xs_hbm.at[pl.ds(t, chunks), :], sem).start()
        return c

    lax.fori_loop(0, tb, start_token, 0, unroll=2)
    for _ in range(TOP_K):
        pltpu.make_async_copy(h2_ref, xs_hbm.at[pl.ds(0, tb * chunks), :], sem).wait()


def _dispatch(pos, tzero, h2_lin, n_tok, d, tm, tb=256):
    chunks = d // LANES
    n_tiles = tzero.shape[0]
    return pl.pallas_call(
        functools.partial(_dispatch_kernel, tb=tb, tm=tm, chunks=chunks, n_tiles=n_tiles),
        grid_spec=pltpu.PrefetchScalarGridSpec(
            num_scalar_prefetch=2,
            grid=(n_tok // tb,),
            in_specs=[pl.BlockSpec((tb * chunks, LANES), lambda i, *_: (i, 0))],
            out_specs=pl.BlockSpec(memory_space=pl.ANY),
            scratch_shapes=[pltpu.VMEM((tm * chunks, LANES), F32),
                            pltpu.SemaphoreType.DMA, pltpu.SemaphoreType.DMA],
        ),
        out_shape=jax.ShapeDtypeStruct((n_tiles * tm * chunks, LANES), F32),
        compiler_params=_cparams(("arbitrary",)),
        name="dispatch",
    )(pos, tzero, h2_lin)


def _expert_kernel(te_ref, tfirst_ref, tvalid_ref, txb_ref,
                   x_ref, w1_ref, b1_ref, w2_ref, b2_ref, y_ref, w2b, *, tm, chunks):
    i = pl.program_id(0)

    @pl.when(tvalid_ref[i] > 0)
    def _():
        @pl.when(tfirst_ref[i] > 0)
        def _():
            w2b[...] = w2_ref[...].astype(BF16)

        x = jnp.concatenate([x_ref[pl.ds(c, tm, stride=chunks), :] for c in range(chunks)],
                            axis=1).astype(BF16)
        u = jnp.dot(x, w1_ref[...], preferred_element_type=F32) + b1_ref[...]
        glu = jnp.minimum(u[:, :D_EXPERT], SWIGLU_LIMIT)
        lin = jnp.clip(u[:, D_EXPERT:], -SWIGLU_LIMIT, SWIGLU_LIMIT)
        a = glu * jax.nn.sigmoid(SWIGLU_ALPHA * glu) * (lin + 1.0)
        y = jnp.dot(a.astype(BF16), w2b[...], preferred_element_type=F32) + b2_ref[...]
        for c in range(chunks):
            y_ref[pl.ds(c, tm, stride=chunks), :] = y[:, c * LANES:(c + 1) * LANES]

    @pl.when(tvalid_ref[i] == 0)
    def _():
        y_ref[...] = jnp.zeros(y_ref.shape, F32)


def _experts(te, tfirst, tvalid, txb, xs_lin, w1p, b1p, w2, b2, layer, d, tm):
    n_tiles = te.shape[0]
    chunks = d // LANES
    e0 = layer * N_EXPERTS
    return pl.pallas_call(
        functools.partial(_expert_kernel, tm=tm, chunks=chunks),
        grid_spec=pltpu.PrefetchScalarGridSpec(
            num_scalar_prefetch=4,
            grid=(n_tiles,),
            in_specs=[
                pl.BlockSpec((tm * chunks, LANES), lambda i, te, tf, tv, txb: (txb[i], 0)),
                pl.BlockSpec((None, d, 2 * D_EXPERT), lambda i, te, *_: (e0 + te[i], 0, 0)),
                pl.BlockSpec((None, 1, 2 * D_EXPERT), lambda i, te, *_: (e0 + te[i], 0, 0)),
                pl.BlockSpec((None, D_EXPERT, d), lambda i, te, *_: (e0 + te[i], 0, 0)),
                pl.BlockSpec((None, 1, d), lambda i, te, *_: (e0 + te[i], 0, 0)),
            ],
            out_specs=pl.BlockSpec((tm * chunks, LANES), lambda i, *_: (i, 0)),
            scratch_shapes=[pltpu.VMEM((D_EXPERT, d), BF16)],
        ),
        out_shape=jax.ShapeDtypeStruct((n_tiles * tm * chunks, LANES), F32),
        compiler_params=_cparams(("arbitrary",)),
        name="experts",
    )(te, tfirst, tvalid, txb, xs_lin, w1p, b1p, w2, b2)


def _combine_kernel(pos_ref, ys_hbm, tw_ref, x_ref, g2_ref, lg_ref, lb_ref,
                    o_ref, gbuf, sem, *, alpha, tm, chunks):
    i = pl.program_id(0)
    n = pl.num_programs(0)
    slot = i % 2

    def issue(step, slot_):
        def token(r, c):
            dst_rows = pl.ds(pl.multiple_of(r * chunks, chunks), chunks)
            base = (step * tm + r) * TOP_K
            for k in range(TOP_K):
                s = pl.multiple_of(pos_ref[base + k] * chunks, chunks)
                pltpu.make_async_copy(ys_hbm.at[pl.ds(s, chunks), :],
                                      gbuf.at[slot_, k, dst_rows, :], sem.at[slot_]).start()
            return c
        lax.fori_loop(0, tm, token, 0, unroll=2)

    @pl.when(i == 0)
    def _():
        issue(0, 0)

    @pl.when(i + 1 < n)
    def _():
        issue(i + 1, 1 - slot)

    for k in range(TOP_K):
        pltpu.make_async_copy(ys_hbm.at[pl.ds(0, tm * chunks), :], gbuf.at[slot, k],
                              sem.at[slot]).wait()

    tw = tw_ref[...]
    cols = []
    for c in range(chunks):
        sl = pl.ds(c, tm, stride=chunks)
        acc = tw[:, 0:1] * gbuf[slot, 0, sl, :]
        for k in range(1, TOP_K):
            acc += tw[:, k:k + 1] * gbuf[slot, k, sl, :]
        cols.append(acc)
    y = jnp.concatenate(cols, axis=1)
    o_ref[...] = _ln(alpha * x_ref[...] + g2_ref[...] * y) * lg_ref[...] + lb_ref[...]


def _combine(pos, ys_lin, top_w, x1, modl, lg, lb, n_rows, n_lat_rows, tm, alpha):
    d = x1.shape[1]
    chunks = d // LANES
    n_lat_tiles = n_lat_rows // tm
    which = lambda i: jnp.where(i >= n_lat_tiles, 1, 0)
    const2 = lambda a: pl.BlockSpec(a.shape, lambda i, *_: (0, 0))
    return pl.pallas_call(
        functools.partial(_combine_kernel, alpha=alpha, tm=tm, chunks=chunks),
        grid_spec=pltpu.PrefetchScalarGridSpec(
            num_scalar_prefetch=1,
            grid=(n_rows // tm,),
            in_specs=[pl.BlockSpec(memory_space=pl.ANY),
                      pl.BlockSpec((tm, LANES), lambda i, *_: (i, 0)),
                      pl.BlockSpec((tm, d), lambda i, *_: (i, 0)),
                      pl.BlockSpec((None, 1, d), lambda i, *_: (which(i) * 6 + 5, 0, 0)),
                      const2(lg), const2(lb)],
            out_specs=pl.BlockSpec((tm, d), lambda i, *_: (i, 0)),
            scratch_shapes=[pltpu.VMEM((2, TOP_K, tm * chunks, LANES), F32),
                            pltpu.SemaphoreType.DMA((2,))],
        ),
        out_shape=jax.ShapeDtypeStruct((n_rows, d), F32),
        compiler_params=_cparams(("arbitrary",)),
        name="combine",
    )(pos, ys_lin, top_w, x1, modl, lg, lb)


def _rope_tables(s_len, n_ctx):
    rows = s_len // GRID_W
    axis_dim = HEAD_DIM // 2
    row = jnp.repeat(jnp.arange(rows, dtype=F32), GRID_W)
    col = jnp.tile(jnp.arange(GRID_W, dtype=F32), rows)
    inv = ROPE_THETA ** (-jnp.arange(0, axis_dim, 2, dtype=F32) / axis_dim)
    ar = row[:, None] * inv[None, :]
    ac = col[:, None] * inv[None, :]
    ang = jnp.concatenate([ar, ar, ac, ac], -1)
    cos = jnp.tile(jnp.cos(ang), (1, 2))
    sin = jnp.tile(jnp.sin(ang), (1, 2))
    lo = (jnp.arange(LANES) % (axis_dim) < axis_dim // 2)[None, :]
    sa = jnp.where(lo, -sin, 0.0)
    sb = jnp.where(lo, 0.0, sin)
    pad = lambda t, v: jnp.concatenate([t, jnp.full((n_ctx, LANES), v, F32)], 0)
    return pad(cos, 1.0), pad(sa, 0.0), pad(sb, 0.0)


def kernel(x, c, ctx, c_ctx, w_ada, b_ada, w_in, q_gain, k_gain, sink, out_gain, w_out,
           ln1_g, ln1_b, w_router, b_router, w1, b1, w2, b2, ln2_g, ln2_b):
    bsz, s_len, d = x.shape
    n_ctx = ctx.shape[1]
    depth = w_ada.shape[0]
    assert bsz == 1 and d % LANES == 0
    nt = s_len + n_ctx
    alpha = (2.0 * depth) ** 0.25
    tm = 256

    condT = jnp.stack([c[0], c_ctx], axis=1)
    mod = _modulation(condT, w_ada, b_ada).reshape(depth, 12, 1, d)
    cos_t, sa_t, sb_t = _rope_tables(s_len, n_ctx)
    hid = jnp.arange(A_Q) // HEAD_DIM
    bd = (hid[:, None] == hid[None, :]).astype(BF16)
    w_in_b = w_in.astype(BF16)
    w_out_b = w_out.astype(BF16)
    n_e = w1.shape[1]
    w1p = _w1_prep(w1)
    b1p = jnp.concatenate([b1[..., 0::2], b1[..., 1::2]], axis=-1).reshape(depth * n_e, 1, -1)
    w2r = w2.reshape(depth * n_e, D_EXPERT, d)
    b2r = b2.reshape(depth * n_e, 1, d)
    wr_p = jnp.pad(w_router, ((0, 0), (0, 0), (0, LANES - N_EXPERTS)))
    wr_hi = wr_p.astype(BF16)
    wr_lo = (wr_p - wr_hi.astype(F32)).astype(BF16)
    br_p = jnp.pad(b_router, ((0, 0), (0, LANES - N_EXPERTS)), constant_values=NEG_BIG)

    x_all = jnp.concatenate([x[0], ctx[0]], axis=0)
    for l in range(depth):
        last = l == depth - 1
        n_rows = s_len if last else nt
        modl = mod[l]
        gq_t = jnp.tile(q_gain[l], N_HEADS)[None, :]
        gk_t = jnp.tile(k_gain[l], KV_HEADS)[None, :]
        qa, kaT, va, qb, kbT, vb = _project(x_all, modl, w_in_b, l, gq_t, gk_t, bd,
                                            cos_t, sa_t, sb_t, s_len, tm)
        o_a = _attn_a(sink[l], qa, kaT, va, n_rows, s_len)
        o_b = _attn_b(qb, kbT, vb, n_rows, s_len)
        x1, h2_lin, top_i, top_w, cnt = _merge(
            o_a, o_b, x_all, modl, out_gain[l][None, :], w_out_b, l, ln1_g[l][None, :],
            ln1_b[l][None, :], wr_hi, wr_lo, br_p[l][None, :], n_rows, s_len, tm, alpha)
        pstart, te, tfirst, tvalid, txb, tzero = _tile_meta(cnt, n_rows * TOP_K, tm)
        eid = top_i[:, :TOP_K]
        of_expert = eid[:, :, None] == jnp.arange(N_EXPERTS, dtype=jnp.int32)[None, None, :]
        pos = (jnp.sum(jnp.where(of_expert, pstart[None, None, :], 0), axis=-1)
               + top_i[:, TOP_K:2 * TOP_K]).reshape(-1)
        xs_lin = _dispatch(pos, tzero, h2_lin, n_rows, d, tm)
        ys_lin = _experts(te, tfirst, tvalid, txb, xs_lin, w1p, b1p, w2r, b2r, l, d, tm)
        x_all = _combine(pos, ys_lin, top_w, x1, modl, ln2_g[l][None, :],
                         ln2_b[l][None, :], n_rows, s_len, tm, alpha)
    return x_all[None]
```

```python
import functools

import jax
import jax.numpy as jnp
from jax import lax
from jax.experimental import pallas as pl
from jax.experimental.pallas import tpu as pltpu

F32 = jnp.float32
BF16 = jnp.bfloat16

GRID_W = 64
HEAD_DIM = 64
N_HEADS = 16
KV_HEADS = 2
GROUP = N_HEADS // KV_HEADS
WINDOW = 128
BLOCK = 128
ROPE_THETA = 10000.0
ATTN_SCALE = HEAD_DIM ** -0.5
A_Q = N_HEADS * HEAD_DIM
A_KV = KV_HEADS * HEAD_DIM
N_EXPERTS = 32
TOP_K = 4
D_EXPERT = 512
SWIGLU_LIMIT = 7.0
SWIGLU_ALPHA = 1.702
LN_EPS = 1e-5
RMS_EPS = 1e-6
LOG2_E = 1.4426950408889634
LANES = 128
NEG_BIG = -1e30

VMEM_LIMIT = 56 * 1024 * 1024


def _cparams(sem):
    return pltpu.CompilerParams(dimension_semantics=sem, vmem_limit_bytes=VMEM_LIMIT)


def _ln(x):
    mu = jnp.mean(x, -1, keepdims=True)
    xc = x - mu
    var = jnp.mean(xc * xc, -1, keepdims=True)
    return xc * lax.rsqrt(var + LN_EPS)


def _mod_kernel(condT_ref, w_ref, b_ref, o_ref):
    ct = condT_ref[...]
    s = ct * jax.nn.sigmoid(ct)
    w = w_ref[...]
    r0 = jnp.sum(s[:, 0:1] * w, axis=0, keepdims=True)
    r1 = jnp.sum(s[:, 1:2] * w, axis=0, keepdims=True)
    o_ref[...] = jnp.concatenate([r0, r1], axis=0) + b_ref[...]


def _modulation(condT, w_ada, b_ada):
    depth, d, e = w_ada.shape
    tn = 512
    return pl.pallas_call(
        _mod_kernel,
        grid=(depth, e // tn),
        in_specs=[
            pl.BlockSpec((d, 2), lambda l, j: (0, 0)),
            pl.BlockSpec((None, d, tn), lambda l, j: (l, 0, j)),
            pl.BlockSpec((None, 1, tn), lambda l, j: (l, 0, j)),
        ],
        out_specs=pl.BlockSpec((None, 2, tn), lambda l, j: (l, 0, j)),
        out_shape=jax.ShapeDtypeStruct((depth, 2, e), F32),
        compiler_params=_cparams(("parallel", "parallel")),
        name="modulation",
    )(condT, w_ada, b_ada.reshape(depth, 1, e))


def _proj_kernel(x_ref, sh_ref, sc_ref, w_ref, gq_ref, gk_ref, bd_ref,
                 cos_ref, sa_ref, sb_ref,
                 qa_ref, kaT_ref, va_ref, qb_ref, kbT_ref, vb_ref, ksq_ref):
    x = x_ref[...]
    h = _ln(x) * (1.0 + sc_ref[...]) + sh_ref[...]
    p = jnp.dot(h.astype(BF16), w_ref[...], preferred_element_type=F32)
    cos, sa, sb = cos_ref[...], sa_ref[...], sb_ref[...]
    lane = lax.broadcasted_iota(jnp.int32, cos.shape, 1)

    def rope(t):
        return (t * cos + pltpu.roll(t, LANES - 16, 1) * sa
                + pltpu.roll(t, 16, 1) * sb)

    def head_rms(t, width):
        sq = t * t
        hi = sq.astype(BF16)
        lo = (sq - hi.astype(F32)).astype(BF16)
        bd = bd_ref[0:width, 0:width]
        ss = (jnp.dot(hi, bd, preferred_element_type=F32)
              + jnp.dot(lo, bd, preferred_element_type=F32))
        return lax.rsqrt(ss * (1.0 / HEAD_DIM) + RMS_EPS)

    def with_ones(t, g):
        src = t if g == 0 else pltpu.roll(t, HEAD_DIM, 1)
        return jnp.where(lane < HEAD_DIM, src,
                         jnp.where(lane == HEAD_DIM, 1.0, 0.0)).astype(BF16)

    def write_q(q_out, q, scale):
        for c in range(A_Q // LANES):
            r = rope(q[:, c * LANES:(c + 1) * LANES]) * scale
            q_out[2 * c] = r[:, :HEAD_DIM].astype(BF16)
            q_out[2 * c + 1] = r[:, HEAD_DIM:].astype(BF16)

    o = 0
    write_q(qa_ref, p[:, o:o + A_Q], ATTN_SCALE); o += A_Q
    kaT_ref[...] = rope(p[:, o:o + A_KV]).T.astype(BF16); o += A_KV
    va = p[:, o:o + A_KV]; o += A_KV
    va_ref[0] = with_ones(va, 0)
    va_ref[1] = with_ones(va, 1)
    qb = p[:, o:o + A_Q]; o += A_Q
    write_q(qb_ref, qb * head_rms(qb, A_Q) * gq_ref[...], ATTN_SCALE * LOG2_E)
    kb = p[:, o:o + A_KV]; o += A_KV
    kb = rope(kb * head_rms(kb, A_KV) * gk_ref[...])
    kbT_ref[...] = kb.T.astype(BF16)
    kq = kb.astype(BF16).astype(F32)
    ksq = kq * kq
    n0 = jnp.sum(jnp.where(lane < HEAD_DIM, ksq, 0.0), axis=1, keepdims=True)
    n1 = jnp.sum(jnp.where(lane >= HEAD_DIM, ksq, 0.0), axis=1, keepdims=True)
    ksq_ref[...] = jnp.where(lane == 0, n0, jnp.where(lane == 1, n1, 0.0))
    vb = p[:, o:o + A_KV]
    vb_ref[0] = with_ones(vb, 0)
    vb_ref[1] = with_ones(vb, 1)


def _project(x_all, modl, w_in_b, layer, gq_t, gk_t, bd, cos_t, sa_t, sb_t, n_lat_rows, tm):
    nt, d = x_all.shape
    n_lat_tiles = n_lat_rows // tm
    which = lambda i: jnp.where(i >= n_lat_tiles, 1, 0)
    modspec = lambda j: pl.BlockSpec((None, 1, d), lambda i: (which(i) * 6 + j, 0, 0))
    const2 = lambda a: pl.BlockSpec(a.shape, lambda i: (0, 0))
    layer3 = lambda a: pl.BlockSpec((None,) + a.shape[1:], lambda i: (layer, 0, 0))
    q_shape = jax.ShapeDtypeStruct((N_HEADS, nt, HEAD_DIM), BF16)
    kT_shape = jax.ShapeDtypeStruct((A_KV, nt), BF16)
    v_shape = jax.ShapeDtypeStruct((KV_HEADS, nt, LANES), BF16)
    q_spec = pl.BlockSpec((N_HEADS, tm, HEAD_DIM), lambda i: (0, i, 0))
    kT_spec = pl.BlockSpec((A_KV, tm), lambda i: (0, i))
    v_spec = pl.BlockSpec((KV_HEADS, tm, LANES), lambda i: (0, i, 0))
    tab_spec = pl.BlockSpec((tm, LANES), lambda i: (i, 0))
    return pl.pallas_call(
        _proj_kernel,
        grid=(nt // tm,),
        in_specs=[pl.BlockSpec((tm, d), lambda i: (i, 0)), modspec(0), modspec(1),
                  layer3(w_in_b), const2(gq_t), const2(gk_t), const2(bd),
                  tab_spec, tab_spec, tab_spec],
        out_specs=[q_spec, kT_spec, v_spec, q_spec, kT_spec, v_spec, tab_spec],
        out_shape=[q_shape, kT_shape, v_shape, q_shape, kT_shape, v_shape,
                   jax.ShapeDtypeStruct((nt, LANES), F32)],
        compiler_params=_cparams(("parallel",)),
        name="project",
    )(x_all, modl, modl, w_in_b, gq_t, gk_t, bd, cos_t, sa_t, sb_t)


FIXED_REF_LIMIT = 50.0


def _attn_b_kernel(kmax_ref, q_ref, kT_ref, v_ref, o_ref, m_scr, acc_scr,
                   *, n_lat_rows, tk, blocks_per_step):
    i = pl.program_id(0)
    tq = q_ref.shape[1]
    nt = kT_ref.shape[1]
    is_ctx = i * tq >= n_lat_rows
    acc_scr[...] = jnp.zeros(acc_scr.shape, F32)

    bound = jnp.zeros((tq, LANES), F32)
    for h in range(N_HEADS):
        qf = q_ref[h].astype(F32)
        qn = jnp.sqrt(jnp.sum(qf * qf, axis=1, keepdims=True))
        m_h = jnp.broadcast_to(qn * kmax_ref[h // GROUP], (tq, LANES))
        m_scr[h] = m_h
        bound = jnp.maximum(bound, m_h)
    fixed_ref_ok = jnp.max(bound) <= FIXED_REF_LIMIT

    def keys_fixed(off, width):
        for h in range(N_HEADS):
            g = h // GROUP
            kt = kT_ref[g * HEAD_DIM:(g + 1) * HEAD_DIM, pl.ds(off, width)]
            s = jnp.dot(q_ref[h], kt, preferred_element_type=F32)
            p = jnp.exp2(s - jnp.concatenate([m_scr[h]] * (width // LANES), axis=1))
            acc_scr[h] += jnp.dot(p.astype(BF16), v_ref[g, pl.ds(off, width), :],
                                  preferred_element_type=F32)

    def sweep(update, per_step):
        def step(j, carry):
            for u in range(per_step):
                update(pl.multiple_of(j * (tk * per_step) + u * tk, tk), tk)
            return carry

        lax.fori_loop(0, jnp.where(is_ctx, 0, n_lat_rows // (tk * per_step)), step, 0)
        update(n_lat_rows, nt - n_lat_rows)

    @pl.when(fixed_ref_ok)
    def _():
        sweep(keys_fixed, blocks_per_step[0])

    def keys(off, width):
        for h in range(N_HEADS):
            g = h // GROUP
            kt = kT_ref[g * HEAD_DIM:(g + 1) * HEAD_DIM, pl.ds(off, width)]
            s = jnp.dot(q_ref[h], kt, preferred_element_type=F32)
            m_prev = m_scr[h]
            m_new = jnp.maximum(m_prev, jnp.max(s, axis=1, keepdims=True))
            alpha = jnp.exp2(m_prev - m_new)
            p = jnp.exp2(s - jnp.concatenate([m_new] * (width // LANES), axis=1))
            pv = jnp.dot(p.astype(BF16), v_ref[g, pl.ds(off, width), :],
                         preferred_element_type=F32)
            acc_scr[h] = acc_scr[h] * alpha + pv
            m_scr[h] = m_new

    @pl.when(jnp.logical_not(fixed_ref_ok))
    def _():
        m_scr[...] = jnp.full(m_scr.shape, NEG_BIG, F32)
        sweep(keys, blocks_per_step[1])

    for h in range(N_HEADS):
        acc = acc_scr[h]
        o_ref[:, h * HEAD_DIM:(h + 1) * HEAD_DIM] = (
            acc[:, :HEAD_DIM] / acc[:, HEAD_DIM:HEAD_DIM + 1])


def _attn_b(kmax, q_hm, kT, v2, n_q_rows, n_lat_rows, tq=128, tk=256):
    nt = kT.shape[1]
    n_blocks = n_lat_rows // tk
    blocks_per_step = (8 if n_blocks % 8 == 0 else 2, 2)
    assert n_lat_rows % tk == 0 and all(n_blocks % b == 0 for b in blocks_per_step)
    return pl.pallas_call(
        functools.partial(_attn_b_kernel, n_lat_rows=n_lat_rows, tk=tk,
                          blocks_per_step=blocks_per_step),
        grid_spec=pltpu.PrefetchScalarGridSpec(
            num_scalar_prefetch=1,
            grid=(n_q_rows // tq,),
            in_specs=[pl.BlockSpec((N_HEADS, tq, HEAD_DIM), lambda i, s: (0, i, 0)),
                      pl.BlockSpec((A_KV, nt), lambda i, s: (0, 0)),
                      pl.BlockSpec((KV_HEADS, nt, LANES), lambda i, s: (0, 0, 0))],
            out_specs=pl.BlockSpec((tq, A_Q), lambda i, s: (i, 0)),
            scratch_shapes=[pltpu.VMEM((N_HEADS, tq, LANES), F32),
                            pltpu.VMEM((N_HEADS, tq, LANES), F32)],
        ),
        out_shape=jax.ShapeDtypeStruct((n_q_rows, A_Q), F32),
        compiler_params=_cparams(("parallel",)),
        name="attn_global",
    )(kmax, q_hm, kT, v2)


def _attn_a_kernel(sink_ref, q_ref, kT_ref, v_ref, o_ref, *, n_lat_rows):
    n = pl.program_id(0)
    n_lat_blocks = n_lat_rows // BLOCK
    nt = kT_ref.shape[1]
    is_lat = n < n_lat_blocks
    ii = lax.broadcasted_iota(jnp.int32, (BLOCK, BLOCK), 0)
    jj = lax.broadcasted_iota(jnp.int32, (BLOCK, BLOCK), 1)
    blocks = []
    for d in (-1, 0, 1):
        blk = n + d
        valid = is_lat & (blk >= 0) & (blk < n_lat_blocks)
        off = pl.multiple_of(jnp.clip(blk, 0, n_lat_blocks - 1) * BLOCK, BLOCK)
        rel_ok = (jj >= ii) if d == -1 else ((jj <= ii) if d == 1 else (jj >= 0))
        blocks.append((off, valid & rel_ok))

    for h in range(N_HEADS):
        g = h // GROUP
        q = q_ref[h]
        kg = kT_ref.at[g * HEAD_DIM:(g + 1) * HEAD_DIM, :]
        vg = v_ref.at[g]
        sink = sink_ref[h]
        s_list = []
        for off, mask in blocks:
            s = jnp.dot(q, kg[:, pl.ds(off, BLOCK)], preferred_element_type=F32)
            s_list.append(jnp.where(mask, s, NEG_BIG))
        s_ctx = jnp.dot(q, kg[:, n_lat_rows:nt], preferred_element_type=F32)
        col_max = s_list[0]
        for s in s_list[1:]:
            col_max = jnp.maximum(col_max, s)
        for c in range((nt - n_lat_rows) // BLOCK):
            col_max = jnp.maximum(col_max, s_ctx[:, c * BLOCK:(c + 1) * BLOCK])
        m = jnp.maximum(jnp.max(col_max, axis=1, keepdims=True), sink)
        acc = jnp.dot(jnp.exp(s_ctx - m).astype(BF16), vg[n_lat_rows:nt, :],
                      preferred_element_type=F32)
        for (off, _), s in zip(blocks, s_list):
            acc += jnp.dot(jnp.exp(s - m).astype(BF16), vg[pl.ds(off, BLOCK), :],
                           preferred_element_type=F32)
        denom = acc[:, HEAD_DIM:HEAD_DIM + 1] + jnp.exp(sink - m)
        o_ref[:, h * HEAD_DIM:(h + 1) * HEAD_DIM] = acc[:, :HEAD_DIM] / denom


def _attn_a(sink_l, q_hm, kT, v2, n_q_rows, n_lat_rows):
    nt = kT.shape[1]
    return pl.pallas_call(
        functools.partial(_attn_a_kernel, n_lat_rows=n_lat_rows),
        grid_spec=pltpu.PrefetchScalarGridSpec(
            num_scalar_prefetch=1,
            grid=(n_q_rows // BLOCK,),
            in_specs=[pl.BlockSpec((N_HEADS, BLOCK, HEAD_DIM), lambda i, s: (0, i, 0)),
                      pl.BlockSpec((A_KV, nt), lambda i, s: (0, 0)),
                      pl.BlockSpec((KV_HEADS, nt, LANES), lambda i, s: (0, 0, 0))],
            out_specs=pl.BlockSpec((BLOCK, A_Q), lambda i, s: (i, 0)),
        ),
        out_shape=jax.ShapeDtypeStruct((n_q_rows, A_Q), F32),
        compiler_params=_cparams(("parallel",)),
        name="attn_window",
    )(sink_l, q_hm, kT, v2)


def _w1_prep_kernel(w_ref, se_ref, so_ref, o_ref):
    wb = w_ref[...].astype(BF16)
    half = D_EXPERT // 2
    for b in range(2):
        blk = wb[:, 2 * half * b:2 * half * (b + 1)]
        o_ref[:, half * b:half * (b + 1)] = jnp.dot(
            blk, se_ref[...], preferred_element_type=F32).astype(BF16)
        o_ref[:, D_EXPERT + half * b:D_EXPERT + half * (b + 1)] = jnp.dot(
            blk, so_ref[...], preferred_element_type=F32).astype(BF16)


def _w1_prep(w1):
    depth, n_e, d, n2 = w1.shape
    j = jnp.arange(D_EXPERT)[:, None]
    m = jnp.arange(D_EXPERT // 2)[None, :]
    sel_even = (j == 2 * m).astype(BF16)
    sel_odd = (j == 2 * m + 1).astype(BF16)
    const2 = lambda a: pl.BlockSpec(a.shape, lambda i: (0, 0))
    out = pl.pallas_call(
        _w1_prep_kernel,
        grid=(depth * n_e,),
        in_specs=[pl.BlockSpec((None, d, n2), lambda i: (i, 0, 0)), const2(sel_even), const2(sel_odd)],
        out_specs=pl.BlockSpec((None, d, n2), lambda i: (i, 0, 0)),
        out_shape=jax.ShapeDtypeStruct((depth * n_e, d, n2), BF16),
        compiler_params=_cparams(("parallel",)),
        name="w1_prep",
    )(w1.reshape(depth * n_e, d, n2), sel_even, sel_odd)
    return out


def _merge_kernel(oa_ref, ob_ref, x_ref, og_ref, wout_ref, g1_ref, lg_ref, lb_ref,
                  sh2_ref, sc2_ref, wrh_ref, wrl_ref, br_ref,
                  x1_ref, h2_ref, ti_ref, tw_ref, cnt_ref, *, alpha):
    def rms(t, g):
        return t * lax.rsqrt(jnp.mean(t * t, -1, keepdims=True) + RMS_EPS) * g

    og = og_ref[...]
    o = jnp.concatenate([rms(oa_ref[...], og[:, :A_Q]), rms(ob_ref[...], og[:, A_Q:])], axis=1)
    mix = jnp.dot(o.astype(BF16), wout_ref[...], preferred_element_type=F32)
    x1 = _ln(alpha * x_ref[...] + g1_ref[...] * mix) * lg_ref[...] + lb_ref[...]
    x1_ref[...] = x1
    h2 = _ln(x1) * (1.0 + sc2_ref[...]) + sh2_ref[...]
    tm, d = h2.shape
    chunks = d // LANES
    for c in range(chunks):
        h2_ref[pl.ds(c, tm, stride=chunks), :] = h2[:, c * LANES:(c + 1) * LANES]

    h_hi = h2.astype(BF16)
    h_lo = (h2 - h_hi.astype(F32)).astype(BF16)
    w_hi = wrh_ref[...]
    logits = ((jnp.dot(h_hi, w_hi, preferred_element_type=F32)
               + jnp.dot(h_lo, w_hi, preferred_element_type=F32))
              + jnp.dot(h_hi, wrl_ref[...], preferred_element_type=F32)) + br_ref[...]
    lane = lax.broadcasted_iota(jnp.int32, logits.shape, 1)
    lane_f = lane.astype(F32)
    work = logits
    vals, ids = [], []
    for _ in range(TOP_K):
        mx = jnp.max(work, axis=1, keepdims=True)
        idx = jnp.min(jnp.where(work == mx, lane_f, float(LANES)), axis=1, keepdims=True)
        vals.append(mx)
        ids.append(idx)
        work = jnp.where(lane_f == idx, NEG_BIG, work)
    es = [jnp.exp(v - vals[0]) for v in vals]
    tot = es[0] + es[1] + es[2] + es[3]

    @pl.when(pl.program_id(0) == 0)
    def _():
        cnt_ref[...] = jnp.zeros(cnt_ref.shape, F32)

    onehot = [(lane_f == ids[k]).astype(F32) for k in range(TOP_K)]
    picked = (onehot[0] + onehot[1]) + (onehot[2] + onehot[3])
    earlier = (lax.broadcasted_iota(jnp.int32, (tm, tm), 1)
               < lax.broadcasted_iota(jnp.int32, (tm, tm), 0)).astype(BF16)
    before = jnp.dot(earlier, picked.astype(BF16), preferred_element_type=F32) + cnt_ref[...]
    ranks = [jnp.sum(onehot[k] * before, axis=1, keepdims=True) for k in range(TOP_K)]
    cnt_ref[...] += jnp.sum(picked, axis=0, keepdims=True)

    ti = jnp.zeros(logits.shape, jnp.int32)
    tw = jnp.zeros(logits.shape, F32)
    for k in range(TOP_K):
        ti = jnp.where(lane == k, ids[k].astype(jnp.int32), ti)
        ti = jnp.where(lane == TOP_K + k, ranks[k].astype(jnp.int32), ti)
        tw = jnp.where(lane == k, es[k] / tot, tw)
    ti_ref[...] = ti
    tw_ref[...] = tw


def _merge(o_a, o_b, x_all, modl, og, w_out_b, layer, lg, lb, wr_hi, wr_lo, br_p, n_rows,
           n_lat_rows, tm, alpha):
    d = x_all.shape[1]
    chunks = d // LANES
    n_lat_tiles = n_lat_rows // tm
    which = lambda i: jnp.where(i >= n_lat_tiles, 1, 0)
    modspec = lambda j: pl.BlockSpec((None, 1, d), lambda i: (which(i) * 6 + j, 0, 0))
    const2 = lambda a: pl.BlockSpec(a.shape, lambda i: (0, 0))
    layer3 = lambda a: pl.BlockSpec((None,) + a.shape[1:], lambda i: (layer, 0, 0))
    row = lambda w: pl.BlockSpec((tm, w), lambda i: (i, 0))
    return pl.pallas_call(
        functools.partial(_merge_kernel, alpha=alpha),
        grid=(n_rows // tm,),
        in_specs=[row(A_Q), row(A_Q), row(d), const2(og), layer3(w_out_b), modspec(2),
                  const2(lg), const2(lb), modspec(3), modspec(4), layer3(wr_hi), layer3(wr_lo),
                  const2(br_p)],
        out_specs=[row(d), pl.BlockSpec((tm * chunks, LANES), lambda i: (i, 0)),
                   row(LANES), row(LANES), pl.BlockSpec((1, LANES), lambda i: (0, 0))],
        out_shape=[jax.ShapeDtypeStruct((n_rows, d), F32),
                   jax.ShapeDtypeStruct((n_rows * chunks, LANES), F32),
                   jax.ShapeDtypeStruct((n_rows, LANES), jnp.int32),
                   jax.ShapeDtypeStruct((n_rows, LANES), F32),
                   jax.ShapeDtypeStruct((1, LANES), F32)],
        compiler_params=_cparams(("arbitrary",)),
        name="merge_route",
    )(o_a, o_b, x_all, og, w_out_b, modl, lg, lb, modl, modl, wr_hi, wr_lo, br_p)


def _tile_meta(cnt, n_flat, tm):
    i32 = jnp.int32
    counts = cnt[0, :N_EXPERTS].astype(i32)
    n_tiles = n_flat // tm + N_EXPERTS
    tiles_per = (counts + tm - 1) // tm
    tile_end = jnp.cumsum(tiles_per).astype(i32)
    tile_start = tile_end - tiles_per
    tid = jnp.arange(n_tiles, dtype=i32)
    used = tid < tile_end[-1]
    te = jnp.minimum(jnp.sum((tile_end[None, :] <= tid[:, None]).astype(i32), axis=1), N_EXPERTS - 1)
    onehot = te[:, None] == jnp.arange(N_EXPERTS, dtype=i32)[None, :]
    cnt_t = jnp.sum(jnp.where(onehot, counts[None, :], 0), axis=1)
    start_t = jnp.sum(jnp.where(onehot, tile_start[None, :], 0), axis=1)
    tvalid = jnp.where(used, jnp.clip(cnt_t - (tid - start_t) * tm, 0, tm), 0).astype(i32)
    tfirst = (used & (tid == start_t)).astype(i32)
    txb = jnp.where(used, tid, tile_end[-1] - 1).astype(i32)
    tzero = (tvalid < tm).astype(i32)
    return (tile_start * tm).astype(i32), te.astype(i32), tfirst, tvalid, txb, tzero


def _dispatch_kernel(pos_ref, tzero_ref, h2_ref, xs_hbm, zbuf, sem, zsem,
                     *, tb, tm, chunks, n_tiles):
    i = pl.program_id(0)

    @pl.when(i == 0)
    def _():
        zbuf[...] = jnp.zeros(zbuf.shape, F32)

        def fill(t):
            off = pl.multiple_of(t * (tm * chunks), tm * chunks)
            return pltpu.make_async_copy(zbuf, xs_hbm.at[pl.ds(off, tm * chunks), :], zsem)

        def start(t, c):
            @pl.when(tzero_ref[t] > 0)
            def _():
                fill(t).start()
            return c

        def wait(t, c):
            @pl.when(tzero_ref[t] > 0)
            def _():
                fill(t).wait()
            return c

        lax.fori_loop(0, n_tiles, start, 0)
        lax.fori_loop(0, n_tiles, wait, 0)

    def start_token(r, c):
        src = h2_ref.at[pl.ds(pl.multiple_of(r * chunks, chunks), chunks), :]
        base = (i * tb + r) * TOP_K
        for k in range(TOP_K):
            t = pl.multiple_of(pos_ref[base + k] * chunks, chunks)
            pltpu.make_async_copy(src, xs_hbm.at[pl.ds(t, chunks), :], sem).start()
        return c

    lax.fori_loop(0, tb, start_token, 0, unroll=2)
    for _ in range(TOP_K):
        pltpu.make_async_copy(h2_ref, xs_hbm.at[pl.ds(0, tb * chunks), :], sem).wait()


def _dispatch(pos, tzero, h2_lin, n_tok, d, tm, tb=256):
    chunks = d // LANES
    n_tiles = tzero.shape[0]
    return pl.pallas_call(
        functools.partial(_dispatch_kernel, tb=tb, tm=tm, chunks=chunks, n_tiles=n_tiles),
        grid_spec=pltpu.PrefetchScalarGridSpec(
            num_scalar_prefetch=2,
            grid=(n_tok // tb,),
            in_specs=[pl.BlockSpec((tb * chunks, LANES), lambda i, *_: (i, 0))],
            out_specs=pl.BlockSpec(memory_space=pl.ANY),
            scratch_shapes=[pltpu.VMEM((tm * chunks, LANES), F32),
                            pltpu.SemaphoreType.DMA, pltpu.SemaphoreType.DMA],
        ),
        out_shape=jax.ShapeDtypeStruct((n_tiles * tm * chunks, LANES), F32),
        compiler_params=_cparams(("arbitrary",)),
        name="dispatch",
    )(pos, tzero, h2_lin)


def _expert_kernel(te_ref, tfirst_ref, tvalid_ref, txb_ref,
                   x_ref, w1_ref, b1_ref, w2_ref, b2_ref, y_ref, w2b, *, tm, chunks):
    i = pl.program_id(0)

    @pl.when(tvalid_ref[i] > 0)
    def _():
        @pl.when(tfirst_ref[i] > 0)
        def _():
            w2b[...] = w2_ref[...].astype(BF16)

        x = jnp.concatenate([x_ref[pl.ds(c, tm, stride=chunks), :] for c in range(chunks)],
                            axis=1).astype(BF16)
        u = jnp.dot(x, w1_ref[...], preferred_element_type=F32) + b1_ref[...]
        glu = jnp.minimum(u[:, :D_EXPERT], SWIGLU_LIMIT)
        lin = jnp.clip(u[:, D_EXPERT:], -SWIGLU_LIMIT, SWIGLU_LIMIT)
        a = glu * jax.nn.sigmoid(SWIGLU_ALPHA * glu) * (lin + 1.0)
        y = jnp.dot(a.astype(BF16), w2b[...], preferred_element_type=F32) + b2_ref[...]
        for c in range(chunks):
            y_ref[pl.ds(c, tm, stride=chunks), :] = y[:, c * LANES:(c + 1) * LANES]

    @pl.when(tvalid_ref[i] == 0)
    def _():
        y_ref[...] = jnp.zeros(y_ref.shape, F32)


def _experts(te, tfirst, tvalid, txb, xs_lin, w1p, b1p, w2, b2, layer, d, tm):
    n_tiles = te.shape[0]
    chunks = d // LANES
    e0 = layer * N_EXPERTS
    return pl.pallas_call(
        functools.partial(_expert_kernel, tm=tm, chunks=chunks),
        grid_spec=pltpu.PrefetchScalarGridSpec(
            num_scalar_prefetch=4,
            grid=(n_tiles,),
            in_specs=[
                pl.BlockSpec((tm * chunks, LANES), lambda i, te, tf, tv, txb: (txb[i], 0)),
                pl.BlockSpec((None, d, 2 * D_EXPERT), lambda i, te, *_: (e0 + te[i], 0, 0)),
                pl.BlockSpec((None, 1, 2 * D_EXPERT), lambda i, te, *_: (e0 + te[i], 0, 0)),
                pl.BlockSpec((None, D_EXPERT, d), lambda i, te, *_: (e0 + te[i], 0, 0)),
                pl.BlockSpec((None, 1, d), lambda i, te, *_: (e0 + te[i], 0, 0)),
            ],
            out_specs=pl.BlockSpec((tm * chunks, LANES), lambda i, *_: (i, 0)),
            scratch_shapes=[pltpu.VMEM((D_EXPERT, d), BF16)],
        ),
        out_shape=jax.ShapeDtypeStruct((n_tiles * tm * chunks, LANES), F32),
        compiler_params=_cparams(("arbitrary",)),
        name="experts",
    )(te, tfirst, tvalid, txb, xs_lin, w1p, b1p, w2, b2)


def _combine_kernel(pos_ref, ys_hbm, tw_ref, x_ref, g2_ref, lg_ref, lb_ref,
                    o_ref, gbuf, sem, *, alpha, tm, chunks):
    i = pl.program_id(0)
    n = pl.num_programs(0)
    slot = i % 2

    def issue(step, slot_):
        def token(r, c):
            dst_rows = pl.ds(pl.multiple_of(r * chunks, chunks), chunks)
            base = (step * tm + r) * TOP_K
            for k in range(TOP_K):
                s = pl.multiple_of(pos_ref[base + k] * chunks, chunks)
                pltpu.make_async_copy(ys_hbm.at[pl.ds(s, chunks), :],
                                      gbuf.at[slot_, k, dst_rows, :], sem.at[slot_]).start()
            return c
        lax.fori_loop(0, tm, token, 0, unroll=2)

    @pl.when(i == 0)
    def _():
        issue(0, 0)

    @pl.when(i + 1 < n)
    def _():
        issue(i + 1, 1 - slot)

    for k in range(TOP_K):
        pltpu.make_async_copy(ys_hbm.at[pl.ds(0, tm * chunks), :], gbuf.at[slot, k],
                              sem.at[slot]).wait()

    tw = tw_ref[...]
    cols = []
    for c in range(chunks):
        sl = pl.ds(c, tm, stride=chunks)
        acc = tw[:, 0:1] * gbuf[slot, 0, sl, :]
        for k in range(1, TOP_K):
            acc += tw[:, k:k + 1] * gbuf[slot, k, sl, :]
        cols.append(acc)
    y = jnp.concatenate(cols, axis=1)
    o_ref[...] = _ln(alpha * x_ref[...] + g2_ref[...] * y) * lg_ref[...] + lb_ref[...]


def _combine(pos, ys_lin, top_w, x1, modl, lg, lb, n_rows, n_lat_rows, tm, alpha):
    d = x1.shape[1]
    chunks = d // LANES
    n_lat_tiles = n_lat_rows // tm
    which = lambda i: jnp.where(i >= n_lat_tiles, 1, 0)
    const2 = lambda a: pl.BlockSpec(a.shape, lambda i, *_: (0, 0))
    return pl.pallas_call(
        functools.partial(_combine_kernel, alpha=alpha, tm=tm, chunks=chunks),
        grid_spec=pltpu.PrefetchScalarGridSpec(
            num_scalar_prefetch=1,
            grid=(n_rows // tm,),
            in_specs=[pl.BlockSpec(memory_space=pl.ANY),
                      pl.BlockSpec((tm, LANES), lambda i, *_: (i, 0)),
                      pl.BlockSpec((tm, d), lambda i, *_: (i, 0)),
                      pl.BlockSpec((None, 1, d), lambda i, *_: (which(i) * 6 + 5, 0, 0)),
                      const2(lg), const2(lb)],
            out_specs=pl.BlockSpec((tm, d), lambda i, *_: (i, 0)),
            scratch_shapes=[pltpu.VMEM((2, TOP_K, tm * chunks, LANES), F32),
                            pltpu.SemaphoreType.DMA((2,))],
        ),
        out_shape=jax.ShapeDtypeStruct((n_rows, d), F32),
        compiler_params=_cparams(("arbitrary",)),
        name="combine",
    )(pos, ys_lin, top_w, x1, modl, lg, lb)


def _rope_tables(s_len, n_ctx):
    rows = s_len // GRID_W
    axis_dim = HEAD_DIM // 2
    row = jnp.repeat(jnp.arange(rows, dtype=F32), GRID_W)
    col = jnp.tile(jnp.arange(GRID_W, dtype=F32), rows)
    inv = ROPE_THETA ** (-jnp.arange(0, axis_dim, 2, dtype=F32) / axis_dim)
    ar = row[:, None] * inv[None, :]
    ac = col[:, None] * inv[None, :]
    ang = jnp.concatenate([ar, ar, ac, ac], -1)
    cos = jnp.tile(jnp.cos(ang), (1, 2))
    sin = jnp.tile(jnp.sin(ang), (1, 2))
    lo = (jnp.arange(LANES) % (axis_dim) < axis_dim // 2)[None, :]
    sa = jnp.where(lo, -sin, 0.0)
    sb = jnp.where(lo, 0.0, sin)
    pad = lambda t, v: jnp.concatenate([t, jnp.full((n_ctx, LANES), v, F32)], 0)
    return pad(cos, 1.0), pad(sa, 0.0), pad(sb, 0.0)


def kernel(x, c, ctx, c_ctx, w_ada, b_ada, w_in, q_gain, k_gain, sink, out_gain, w_out,
           ln1_g, ln1_b, w_router, b_router, w1, b1, w2, b2, ln2_g, ln2_b):
    bsz, s_len, d = x.shape
    n_ctx = ctx.shape[1]
    depth = w_ada.shape[0]
    assert bsz == 1 and d % LANES == 0
    nt = s_len + n_ctx
    alpha = (2.0 * depth) ** 0.25
    tm = 256

    condT = jnp.stack([c[0], c_ctx], axis=1)
    mod = _modulation(condT, w_ada, b_ada).reshape(depth, 12, 1, d)
    cos_t, sa_t, sb_t = _rope_tables(s_len, n_ctx)
    hid = jnp.arange(A_Q) // HEAD_DIM
    bd = (hid[:, None] == hid[None, :]).astype(BF16)
    w_in_b = w_in.astype(BF16)
    w_out_b = w_out.astype(BF16)
    n_e = w1.shape[1]
    w1p = _w1_prep(w1)
    b1p = jnp.concatenate([b1[..., 0::2], b1[..., 1::2]], axis=-1).reshape(depth * n_e, 1, -1)
    w2r = w2.reshape(depth * n_e, D_EXPERT, d)
    b2r = b2.reshape(depth * n_e, 1, d)
    wr_p = jnp.pad(w_router, ((0, 0), (0, 0), (0, LANES - N_EXPERTS)))
    wr_hi = wr_p.astype(BF16)
    wr_lo = (wr_p - wr_hi.astype(F32)).astype(BF16)
    br_p = jnp.pad(b_router, ((0, 0), (0, LANES - N_EXPERTS)), constant_values=NEG_BIG)

    x_all = jnp.concatenate([x[0], ctx[0]], axis=0)
    for l in range(depth):
        last = l == depth - 1
        n_rows = s_len if last else nt
        modl = mod[l]
        gq_t = jnp.tile(q_gain[l], N_HEADS)[None, :]
        gk_t = jnp.tile(k_gain[l], KV_HEADS)[None, :]
        qa, kaT, va, qb, kbT, vb, ksq = _project(x_all, modl, w_in_b, l, gq_t, gk_t, bd,
                                                 cos_t, sa_t, sb_t, s_len, tm)
        o_a = _attn_a(sink[l], qa, kaT, va, n_rows, s_len)
        kmax = jnp.sqrt(jnp.max(ksq[:, :KV_HEADS], axis=0))
        o_b = _attn_b(kmax, qb, kbT, vb, n_rows, s_len)
        x1, h2_lin, top_i, top_w, cnt = _merge(
            o_a, o_b, x_all, modl, out_gain[l][None, :], w_out_b, l, ln1_g[l][None, :],
            ln1_b[l][None, :], wr_hi, wr_lo, br_p[l][None, :], n_rows, s_len, tm, alpha)
        pstart, te, tfirst, tvalid, txb, tzero = _tile_meta(cnt, n_rows * TOP_K, tm)
        eid = top_i[:, :TOP_K]
        of_expert = eid[:, :, None] == jnp.arange(N_EXPERTS, dtype=jnp.int32)[None, None, :]
        pos = (jnp.sum(jnp.where(of_expert, pstart[None, None, :], 0), axis=-1)
               + top_i[:, TOP_K:2 * TOP_K]).reshape(-1)
        xs_lin = _dispatch(pos, tzero, h2_lin, n_rows, d, tm)
        ys_lin = _experts(te, tfirst, tvalid, txb, xs_lin, w1p, b1p, w2r, b2r, l, d, tm)
        x_all = _combine(pos, ys_lin, top_w, x1, modl, ln2_g[l][None, :],
                         ln2_b[l][None, :], n_rows, s_len, tm, alpha)
    return x_all[None]
```

```python
import functools

import jax
import jax.numpy as jnp
from jax import lax
from jax.experimental import pallas as pl
from jax.experimental.pallas import tpu as pltpu

F32 = jnp.float32
BF16 = jnp.bfloat16

GRID_W = 64
HEAD_DIM = 64
N_HEADS = 16
KV_HEADS = 2
GROUP = N_HEADS // KV_HEADS
WINDOW = 128
BLOCK = 128
ROPE_THETA = 10000.0
ATTN_SCALE = HEAD_DIM ** -0.5
A_Q = N_HEADS * HEAD_DIM
A_KV = KV_HEADS * HEAD_DIM
N_EXPERTS = 32
TOP_K = 4
D_EXPERT = 512
SWIGLU_LIMIT = 7.0
SWIGLU_ALPHA = 1.702
LN_EPS = 1e-5
RMS_EPS = 1e-6
LOG2_E = 1.4426950408889634
LANES = 128
NEG_BIG = -1e30

VMEM_LIMIT = 56 * 1024 * 1024


def _cparams(sem):
    return pltpu.CompilerParams(dimension_semantics=sem, vmem_limit_bytes=VMEM_LIMIT)


def _ln(x):
    mu = jnp.mean(x, -1, keepdims=True)
    xc = x - mu
    var = jnp.mean(xc * xc, -1, keepdims=True)
    return xc * lax.rsqrt(var + LN_EPS)


def _mod_kernel(condT_ref, w_ref, b_ref, o_ref):
    ct = condT_ref[...]
    s = ct * jax.nn.sigmoid(ct)
    w = w_ref[...]
    r0 = jnp.sum(s[:, 0:1] * w, axis=0, keepdims=True)
    r1 = jnp.sum(s[:, 1:2] * w, axis=0, keepdims=True)
    o_ref[...] = jnp.concatenate([r0, r1], axis=0) + b_ref[...]


def _modulation(condT, w_ada, b_ada):
    depth, d, e = w_ada.shape
    tn = 512
    return pl.pallas_call(
        _mod_kernel,
        grid=(depth, e // tn),
        in_specs=[
            pl.BlockSpec((d, 2), lambda l, j: (0, 0)),
            pl.BlockSpec((None, d, tn), lambda l, j: (l, 0, j)),
            pl.BlockSpec((None, 1, tn), lambda l, j: (l, 0, j)),
        ],
        out_specs=pl.BlockSpec((None, 2, tn), lambda l, j: (l, 0, j)),
        out_shape=jax.ShapeDtypeStruct((depth, 2, e), F32),
        compiler_params=_cparams(("parallel", "parallel")),
        name="modulation",
    )(condT, w_ada, b_ada.reshape(depth, 1, e))


def _proj_kernel(x_ref, sh_ref, sc_ref, w_ref, gq_ref, gk_ref, bd_ref,
                 cos_ref, sa_ref, sb_ref,
                 qa_ref, kaT_ref, va_ref, qb_ref, kbT_ref, vb_ref, ksq_ref):
    x = x_ref[...]
    h = _ln(x) * (1.0 + sc_ref[...]) + sh_ref[...]
    p = jnp.dot(h.astype(BF16), w_ref[...], preferred_element_type=F32)
    cos, sa, sb = cos_ref[...], sa_ref[...], sb_ref[...]
    lane = lax.broadcasted_iota(jnp.int32, cos.shape, 1)

    def rope(t):
        return (t * cos + pltpu.roll(t, LANES - 16, 1) * sa
                + pltpu.roll(t, 16, 1) * sb)

    def head_rms(t, width):
        sq = t * t
        hi = sq.astype(BF16)
        lo = (sq - hi.astype(F32)).astype(BF16)
        bd = bd_ref[0:width, 0:width]
        ss = (jnp.dot(hi, bd, preferred_element_type=F32)
              + jnp.dot(lo, bd, preferred_element_type=F32))
        return lax.rsqrt(ss * (1.0 / HEAD_DIM) + RMS_EPS)

    def with_ones(t, g):
        src = t if g == 0 else pltpu.roll(t, HEAD_DIM, 1)
        return jnp.where(lane < HEAD_DIM, src,
                         jnp.where(lane == HEAD_DIM, 1.0, 0.0)).astype(BF16)

    def write_q(q_out, q, scale):
        for c in range(A_Q // LANES):
            r = rope(q[:, c * LANES:(c + 1) * LANES]) * scale
            q_out[2 * c] = r[:, :HEAD_DIM].astype(BF16)
            q_out[2 * c + 1] = r[:, HEAD_DIM:].astype(BF16)

    o = 0
    write_q(qa_ref, p[:, o:o + A_Q], ATTN_SCALE); o += A_Q
    kaT_ref[...] = rope(p[:, o:o + A_KV]).T.astype(BF16); o += A_KV
    va = p[:, o:o + A_KV]; o += A_KV
    va_ref[0] = with_ones(va, 0)
    va_ref[1] = with_ones(va, 1)
    qb = p[:, o:o + A_Q]; o += A_Q
    write_q(qb_ref, qb * head_rms(qb, A_Q) * gq_ref[...], ATTN_SCALE * LOG2_E)
    kb = p[:, o:o + A_KV]; o += A_KV
    kb = rope(kb * head_rms(kb, A_KV) * gk_ref[...])
    kbT_ref[...] = kb.T.astype(BF16)
    kq = kb.astype(BF16).astype(F32)
    ksq = kq * kq
    n0 = jnp.sum(jnp.where(lane < HEAD_DIM, ksq, 0.0), axis=1, keepdims=True)
    n1 = jnp.sum(jnp.where(lane >= HEAD_DIM, ksq, 0.0), axis=1, keepdims=True)
    ksq_ref[...] = jnp.where(lane == 0, n0, jnp.where(lane == 1, n1, 0.0))
    vb = p[:, o:o + A_KV]
    vb_ref[0] = with_ones(vb, 0)
    vb_ref[1] = with_ones(vb, 1)


def _project(x_all, modl, w_in_b, layer, gq_t, gk_t, bd, cos_t, sa_t, sb_t, n_lat_rows, tm):
    nt, d = x_all.shape
    n_lat_tiles = n_lat_rows // tm
    which = lambda i: jnp.where(i >= n_lat_tiles, 1, 0)
    modspec = lambda j: pl.BlockSpec((None, 1, d), lambda i: (which(i) * 6 + j, 0, 0))
    const2 = lambda a: pl.BlockSpec(a.shape, lambda i: (0, 0))
    layer3 = lambda a: pl.BlockSpec((None,) + a.shape[1:], lambda i: (layer, 0, 0))
    q_shape = jax.ShapeDtypeStruct((N_HEADS, nt, HEAD_DIM), BF16)
    kT_shape = jax.ShapeDtypeStruct((A_KV, nt), BF16)
    v_shape = jax.ShapeDtypeStruct((KV_HEADS, nt, LANES), BF16)
    q_spec = pl.BlockSpec((N_HEADS, tm, HEAD_DIM), lambda i: (0, i, 0))
    kT_spec = pl.BlockSpec((A_KV, tm), lambda i: (0, i))
    v_spec = pl.BlockSpec((KV_HEADS, tm, LANES), lambda i: (0, i, 0))
    tab_spec = pl.BlockSpec((tm, LANES), lambda i: (i, 0))
    return pl.pallas_call(
        _proj_kernel,
        grid=(nt // tm,),
        in_specs=[pl.BlockSpec((tm, d), lambda i: (i, 0)), modspec(0), modspec(1),
                  layer3(w_in_b), const2(gq_t), const2(gk_t), const2(bd),
                  tab_spec, tab_spec, tab_spec],
        out_specs=[q_spec, kT_spec, v_spec, q_spec, kT_spec, v_spec, tab_spec],
        out_shape=[q_shape, kT_shape, v_shape, q_shape, kT_shape, v_shape,
                   jax.ShapeDtypeStruct((nt, LANES), F32)],
        compiler_params=_cparams(("parallel",)),
        name="project",
    )(x_all, modl, modl, w_in_b, gq_t, gk_t, bd, cos_t, sa_t, sb_t)


FIXED_REF_LIMIT = 50.0


def _attn_b_kernel(kmax_ref, q_ref, kT_ref, v_ref, o_ref, m_scr, acc_scr,
                   *, n_lat_rows, tk, blocks_per_step):
    i = pl.program_id(0)
    tq = q_ref.shape[1]
    nt = kT_ref.shape[1]
    is_ctx = i * tq >= n_lat_rows
    acc_scr[...] = jnp.zeros(acc_scr.shape, F32)

    bound = jnp.zeros((tq, LANES), F32)
    for h in range(N_HEADS):
        qf = q_ref[h].astype(F32)
        qn = jnp.sqrt(jnp.sum(qf * qf, axis=1, keepdims=True))
        m_h = jnp.broadcast_to(qn * kmax_ref[h // GROUP], (tq, LANES))
        m_scr[h] = m_h
        bound = jnp.maximum(bound, m_h)
    fixed_ref_ok = jnp.max(bound) <= FIXED_REF_LIMIT

    def keys_fixed(off, width):
        for h in range(N_HEADS):
            g = h // GROUP
            kt = kT_ref[g * HEAD_DIM:(g + 1) * HEAD_DIM, pl.ds(off, width)]
            s = jnp.dot(q_ref[h], kt, preferred_element_type=F32)
            p = jnp.exp2(s - jnp.concatenate([m_scr[h]] * (width // LANES), axis=1))
            acc_scr[h] += jnp.dot(p.astype(BF16), v_ref[g, pl.ds(off, width), :],
                                  preferred_element_type=F32)

    def sweep(update, per_step):
        def step(j, carry):
            for u in range(per_step):
                update(pl.multiple_of(j * (tk * per_step) + u * tk, tk), tk)
            return carry

        lax.fori_loop(0, jnp.where(is_ctx, 0, n_lat_rows // (tk * per_step)), step, 0)
        update(n_lat_rows, nt - n_lat_rows)

    @pl.when(fixed_ref_ok)
    def _():
        per_all = blocks_per_step[2]
        if per_all:
            @pl.when(is_ctx)
            def _():
                keys_fixed(n_lat_rows, tk)

            @pl.when(jnp.logical_not(is_ctx))
            def _():
                def step(j, carry):
                    for u in range(per_all):
                        keys_fixed(pl.multiple_of(j * (tk * per_all) + u * tk, tk), tk)
                    return carry

                lax.fori_loop(0, nt // (tk * per_all), step, 0)
        else:
            sweep(keys_fixed, blocks_per_step[0])

    def keys(off, width):
        for h in range(N_HEADS):
            g = h // GROUP
            kt = kT_ref[g * HEAD_DIM:(g + 1) * HEAD_DIM, pl.ds(off, width)]
            s = jnp.dot(q_ref[h], kt, preferred_element_type=F32)
            m_prev = m_scr[h]
            m_new = jnp.maximum(m_prev, jnp.max(s, axis=1, keepdims=True))
            alpha = jnp.exp2(m_prev - m_new)
            p = jnp.exp2(s - jnp.concatenate([m_new] * (width // LANES), axis=1))
            pv = jnp.dot(p.astype(BF16), v_ref[g, pl.ds(off, width), :],
                         preferred_element_type=F32)
            acc_scr[h] = acc_scr[h] * alpha + pv
            m_scr[h] = m_new

    @pl.when(jnp.logical_not(fixed_ref_ok))
    def _():
        m_scr[...] = jnp.full(m_scr.shape, NEG_BIG, F32)
        sweep(keys, blocks_per_step[1])

    for h in range(N_HEADS):
        acc = acc_scr[h]
        o_ref[:, h * HEAD_DIM:(h + 1) * HEAD_DIM] = (
            acc[:, :HEAD_DIM] / acc[:, HEAD_DIM:HEAD_DIM + 1])


def _attn_b(kmax, q_hm, kT, v2, n_q_rows, n_lat_rows, tq=128, tk=256):
    nt = kT.shape[1]
    n_blocks = n_lat_rows // tk
    per_all = 0
    if nt - n_lat_rows == tk:
        per_all = max(b for b in range(1, 13) if (n_blocks + 1) % b == 0)
        per_all = per_all if per_all >= 3 else 0
    assert n_lat_rows % tk == 0 and n_blocks % 2 == 0
    blocks_per_step = (8 if n_blocks % 8 == 0 else 2, 2, per_all)
    return pl.pallas_call(
        functools.partial(_attn_b_kernel, n_lat_rows=n_lat_rows, tk=tk,
                          blocks_per_step=blocks_per_step),
        grid_spec=pltpu.PrefetchScalarGridSpec(
            num_scalar_prefetch=1,
            grid=(n_q_rows // tq,),
            in_specs=[pl.BlockSpec((N_HEADS, tq, HEAD_DIM), lambda i, s: (0, i, 0)),
                      pl.BlockSpec((A_KV, nt), lambda i, s: (0, 0)),
                      pl.BlockSpec((KV_HEADS, nt, LANES), lambda i, s: (0, 0, 0))],
            out_specs=pl.BlockSpec((tq, A_Q), lambda i, s: (i, 0)),
            scratch_shapes=[pltpu.VMEM((N_HEADS, tq, LANES), F32),
                            pltpu.VMEM((N_HEADS, tq, LANES), F32)],
        ),
        out_shape=jax.ShapeDtypeStruct((n_q_rows, A_Q), F32),
        compiler_params=_cparams(("parallel",)),
        name="attn_global",
    )(kmax, q_hm, kT, v2)


def _attn_a_kernel(sink_ref, q_ref, kT_ref, v_ref, o_ref, *, n_lat_rows):
    n = pl.program_id(0)
    n_lat_blocks = n_lat_rows // BLOCK
    nt = kT_ref.shape[1]
    is_lat = n < n_lat_blocks
    ii = lax.broadcasted_iota(jnp.int32, (BLOCK, BLOCK), 0)
    jj = lax.broadcasted_iota(jnp.int32, (BLOCK, BLOCK), 1)
    blocks = []
    for d in (-1, 0, 1):
        blk = n + d
        valid = is_lat & (blk >= 0) & (blk < n_lat_blocks)
        off = pl.multiple_of(jnp.clip(blk, 0, n_lat_blocks - 1) * BLOCK, BLOCK)
        rel_ok = (jj >= ii) if d == -1 else ((jj <= ii) if d == 1 else (jj >= 0))
        blocks.append((off, valid & rel_ok))

    for h in range(N_HEADS):
        g = h // GROUP
        q = q_ref[h]
        kg = kT_ref.at[g * HEAD_DIM:(g + 1) * HEAD_DIM, :]
        vg = v_ref.at[g]
        sink = sink_ref[h]
        s_list = []
        for off, mask in blocks:
            s = jnp.dot(q, kg[:, pl.ds(off, BLOCK)], preferred_element_type=F32)
            s_list.append(jnp.where(mask, s, NEG_BIG))
        s_ctx = jnp.dot(q, kg[:, n_lat_rows:nt], preferred_element_type=F32)
        col_max = s_list[0]
        for s in s_list[1:]:
            col_max = jnp.maximum(col_max, s)
        for c in range((nt - n_lat_rows) // BLOCK):
            col_max = jnp.maximum(col_max, s_ctx[:, c * BLOCK:(c + 1) * BLOCK])
        m = jnp.maximum(jnp.max(col_max, axis=1, keepdims=True), sink)
        acc = jnp.dot(jnp.exp(s_ctx - m).astype(BF16), vg[n_lat_rows:nt, :],
                      preferred_element_type=F32)
        for (off, _), s in zip(blocks, s_list):
            acc += jnp.dot(jnp.exp(s - m).astype(BF16), vg[pl.ds(off, BLOCK), :],
                           preferred_element_type=F32)
        denom = acc[:, HEAD_DIM:HEAD_DIM + 1] + jnp.exp(sink - m)
        o_ref[:, h * HEAD_DIM:(h + 1) * HEAD_DIM] = acc[:, :HEAD_DIM] / denom


def _attn_a(sink_l, q_hm, kT, v2, n_q_rows, n_lat_rows):
    nt = kT.shape[1]
    return pl.pallas_call(
        functools.partial(_attn_a_kernel, n_lat_rows=n_lat_rows),
        grid_spec=pltpu.PrefetchScalarGridSpec(
            num_scalar_prefetch=1,
            grid=(n_q_rows // BLOCK,),
            in_specs=[pl.BlockSpec((N_HEADS, BLOCK, HEAD_DIM), lambda i, s: (0, i, 0)),
                      pl.BlockSpec((A_KV, nt), lambda i, s: (0, 0)),
                      pl.BlockSpec((KV_HEADS, nt, LANES), lambda i, s: (0, 0, 0))],
            out_specs=pl.BlockSpec((BLOCK, A_Q), lambda i, s: (i, 0)),
        ),
        out_shape=jax.ShapeDtypeStruct((n_q_rows, A_Q), F32),
        compiler_params=_cparams(("parallel",)),
        name="attn_window",
    )(sink_l, q_hm, kT, v2)


def _w1_prep_kernel(w_ref, se_ref, so_ref, o_ref):
    wb = w_ref[...].astype(BF16)
    half = D_EXPERT // 2
    for b in range(2):
        blk = wb[:, 2 * half * b:2 * half * (b + 1)]
        o_ref[:, half * b:half * (b + 1)] = jnp.dot(
            blk, se_ref[...], preferred_element_type=F32).astype(BF16)
        o_ref[:, D_EXPERT + half * b:D_EXPERT + half * (b + 1)] = jnp.dot(
            blk, so_ref[...], preferred_element_type=F32).astype(BF16)


def _w1_prep(w1):
    depth, n_e, d, n2 = w1.shape
    j = jnp.arange(D_EXPERT)[:, None]
    m = jnp.arange(D_EXPERT // 2)[None, :]
    sel_even = (j == 2 * m).astype(BF16)
    sel_odd = (j == 2 * m + 1).astype(BF16)
    const2 = lambda a: pl.BlockSpec(a.shape, lambda i: (0, 0))
    out = pl.pallas_call(
        _w1_prep_kernel,
        grid=(depth * n_e,),
        in_specs=[pl.BlockSpec((None, d, n2), lambda i: (i, 0, 0)), const2(sel_even), const2(sel_odd)],
        out_specs=pl.BlockSpec((None, d, n2), lambda i: (i, 0, 0)),
        out_shape=jax.ShapeDtypeStruct((depth * n_e, d, n2), BF16),
        compiler_params=_cparams(("parallel",)),
        name="w1_prep",
    )(w1.reshape(depth * n_e, d, n2), sel_even, sel_odd)
    return out


def _merge_kernel(oa_ref, ob_ref, x_ref, og_ref, wout_ref, g1_ref, lg_ref, lb_ref,
                  sh2_ref, sc2_ref, wrh_ref, wrl_ref, br_ref,
                  x1_ref, h2_ref, ti_ref, tw_ref, cnt_ref, *, alpha):
    def rms(t, g):
        return t * lax.rsqrt(jnp.mean(t * t, -1, keepdims=True) + RMS_EPS) * g

    og = og_ref[...]
    o = jnp.concatenate([rms(oa_ref[...], og[:, :A_Q]), rms(ob_ref[...], og[:, A_Q:])], axis=1)
    mix = jnp.dot(o.astype(BF16), wout_ref[...], preferred_element_type=F32)
    x1 = _ln(alpha * x_ref[...] + g1_ref[...] * mix) * lg_ref[...] + lb_ref[...]
    x1_ref[...] = x1
    h2 = _ln(x1) * (1.0 + sc2_ref[...]) + sh2_ref[...]
    tm, d = h2.shape
    chunks = d // LANES
    for c in range(chunks):
        h2_ref[pl.ds(c, tm, stride=chunks), :] = h2[:, c * LANES:(c + 1) * LANES]

    h_hi = h2.astype(BF16)
    h_lo = (h2 - h_hi.astype(F32)).astype(BF16)
    w_hi = wrh_ref[...]
    logits = ((jnp.dot(h_hi, w_hi, preferred_element_type=F32)
               + jnp.dot(h_lo, w_hi, preferred_element_type=F32))
              + jnp.dot(h_hi, wrl_ref[...], preferred_element_type=F32)) + br_ref[...]
    lane = lax.broadcasted_iota(jnp.int32, logits.shape, 1)
    lane_f = lane.astype(F32)
    work = logits
    vals, ids = [], []
    for _ in range(TOP_K):
        mx = jnp.max(work, axis=1, keepdims=True)
        idx = jnp.min(jnp.where(work == mx, lane_f, float(LANES)), axis=1, keepdims=True)
        vals.append(mx)
        ids.append(idx)
        work = jnp.where(lane_f == idx, NEG_BIG, work)
    es = [jnp.exp(v - vals[0]) for v in vals]
    tot = es[0] + es[1] + es[2] + es[3]

    @pl.when(pl.program_id(0) == 0)
    def _():
        cnt_ref[...] = jnp.zeros(cnt_ref.shape, F32)

    onehot = [(lane_f == ids[k]).astype(F32) for k in range(TOP_K)]
    picked = (onehot[0] + onehot[1]) + (onehot[2] + onehot[3])
    earlier = (lax.broadcasted_iota(jnp.int32, (tm, tm), 1)
               < lax.broadcasted_iota(jnp.int32, (tm, tm), 0)).astype(BF16)
    before = jnp.dot(earlier, picked.astype(BF16), preferred_element_type=F32) + cnt_ref[...]
    ranks = [jnp.sum(onehot[k] * before, axis=1, keepdims=True) for k in range(TOP_K)]
    cnt_ref[...] += jnp.sum(picked, axis=0, keepdims=True)

    ti = jnp.zeros(logits.shape, jnp.int32)
    tw = jnp.zeros(logits.shape, F32)
    for k in range(TOP_K):
        ti = jnp.where(lane == k, ids[k].astype(jnp.int32), ti)
        ti = jnp.where(lane == TOP_K + k, ranks[k].astype(jnp.int32), ti)
        tw = jnp.where(lane == k, es[k] / tot, tw)
    ti_ref[...] = ti
    tw_ref[...] = tw


def _merge(o_a, o_b, x_all, modl, og, w_out_b, layer, lg, lb, wr_hi, wr_lo, br_p, n_rows,
           n_lat_rows, tm, alpha):
    d = x_all.shape[1]
    chunks = d // LANES
    n_lat_tiles = n_lat_rows // tm
    which = lambda i: jnp.where(i >= n_lat_tiles, 1, 0)
    modspec = lambda j: pl.BlockSpec((None, 1, d), lambda i: (which(i) * 6 + j, 0, 0))
    const2 = lambda a: pl.BlockSpec(a.shape, lambda i: (0, 0))
    layer3 = lambda a: pl.BlockSpec((None,) + a.shape[1:], lambda i: (layer, 0, 0))
    row = lambda w: pl.BlockSpec((tm, w), lambda i: (i, 0))
    return pl.pallas_call(
        functools.partial(_merge_kernel, alpha=alpha),
        grid=(n_rows // tm,),
        in_specs=[row(A_Q), row(A_Q), row(d), const2(og), layer3(w_out_b), modspec(2),
                  const2(lg), const2(lb), modspec(3), modspec(4), layer3(wr_hi), layer3(wr_lo),
                  const2(br_p)],
        out_specs=[row(d), pl.BlockSpec((tm * chunks, LANES), lambda i: (i, 0)),
                   row(LANES), row(LANES), pl.BlockSpec((1, LANES), lambda i: (0, 0))],
        out_shape=[jax.ShapeDtypeStruct((n_rows, d), F32),
                   jax.ShapeDtypeStruct((n_rows * chunks, LANES), F32),
                   jax.ShapeDtypeStruct((n_rows, LANES), jnp.int32),
                   jax.ShapeDtypeStruct((n_rows, LANES), F32),
                   jax.ShapeDtypeStruct((1, LANES), F32)],
        compiler_params=_cparams(("arbitrary",)),
        name="merge_route",
    )(o_a, o_b, x_all, og, w_out_b, modl, lg, lb, modl, modl, wr_hi, wr_lo, br_p)


def _tile_meta(cnt, n_flat, tm):
    i32 = jnp.int32
    counts = cnt[0, :N_EXPERTS].astype(i32)
    n_tiles = n_flat // tm + N_EXPERTS
    tiles_per = (counts + tm - 1) // tm
    tile_end = jnp.cumsum(tiles_per).astype(i32)
    tile_start = tile_end - tiles_per
    tid = jnp.arange(n_tiles, dtype=i32)
    used = tid < tile_end[-1]
    te = jnp.minimum(jnp.sum((tile_end[None, :] <= tid[:, None]).astype(i32), axis=1), N_EXPERTS - 1)
    onehot = te[:, None] == jnp.arange(N_EXPERTS, dtype=i32)[None, :]
    cnt_t = jnp.sum(jnp.where(onehot, counts[None, :], 0), axis=1)
    start_t = jnp.sum(jnp.where(onehot, tile_start[None, :], 0), axis=1)
    tvalid = jnp.where(used, jnp.clip(cnt_t - (tid - start_t) * tm, 0, tm), 0).astype(i32)
    tfirst = (used & (tid == start_t)).astype(i32)
    txb = jnp.where(used, tid, tile_end[-1] - 1).astype(i32)
    tzero = (tvalid < tm).astype(i32)
    return (tile_start * tm).astype(i32), te.astype(i32), tfirst, tvalid, txb, tzero


def _dispatch_kernel(pos_ref, tzero_ref, h2_ref, xs_hbm, zbuf, sem, zsem,
                     *, tb, tm, chunks, n_tiles):
    i = pl.program_id(0)

    @pl.when(i == 0)
    def _():
        zbuf[...] = jnp.zeros(zbuf.shape, F32)

        def fill(t):
            off = pl.multiple_of(t * (tm * chunks), tm * chunks)
            return pltpu.make_async_copy(zbuf, xs_hbm.at[pl.ds(off, tm * chunks), :], zsem)

        def start(t, c):
            @pl.when(tzero_ref[t] > 0)
            def _():
                fill(t).start()
            return c

        def wait(t, c):
            @pl.when(tzero_ref[t] > 0)
            def _():
                fill(t).wait()
            return c

        lax.fori_loop(0, n_tiles, start, 0)
        lax.fori_loop(0, n_tiles, wait, 0)

    def start_token(r, c):
        src = h2_ref.at[pl.ds(pl.multiple_of(r * chunks, chunks), chunks), :]
        base = (i * tb + r) * TOP_K
        for k in range(TOP_K):
            t = pl.multiple_of(pos_ref[base + k] * chunks, chunks)
            pltpu.make_async_copy(src, xs_hbm.at[pl.ds(t, chunks), :], sem).start()
        return c

    lax.fori_loop(0, tb, start_token, 0, unroll=2)
    for _ in range(TOP_K):
        pltpu.make_async_copy(h2_ref, xs_hbm.at[pl.ds(0, tb * chunks), :], sem).wait()


def _dispatch(pos, tzero, h2_lin, n_tok, d, tm, tb=256):
    chunks = d // LANES
    n_tiles = tzero.shape[0]
    return pl.pallas_call(
        functools.partial(_dispatch_kernel, tb=tb, tm=tm, chunks=chunks, n_tiles=n_tiles),
        grid_spec=pltpu.PrefetchScalarGridSpec(
            num_scalar_prefetch=2,
            grid=(n_tok // tb,),
            in_specs=[pl.BlockSpec((tb * chunks, LANES), lambda i, *_: (i, 0))],
            out_specs=pl.BlockSpec(memory_space=pl.ANY),
            scratch_shapes=[pltpu.VMEM((tm * chunks, LANES), F32),
                            pltpu.SemaphoreType.DMA, pltpu.SemaphoreType.DMA],
        ),
        out_shape=jax.ShapeDtypeStruct((n_tiles * tm * chunks, LANES), F32),
        compiler_params=_cparams(("arbitrary",)),
        name="dispatch",
    )(pos, tzero, h2_lin)


def _expert_kernel(te_ref, tfirst_ref, tvalid_ref, txb_ref,
                   x_ref, w1_ref, b1_ref, w2_ref, b2_ref, y_ref, w2b, *, tm, chunks):
    i = pl.program_id(0)

    @pl.when(tvalid_ref[i] > 0)
    def _():
        @pl.when(tfirst_ref[i] > 0)
        def _():
            w2b[...] = w2_ref[...].astype(BF16)

        x = jnp.concatenate([x_ref[pl.ds(c, tm, stride=chunks), :] for c in range(chunks)],
                            axis=1).astype(BF16)
        u = jnp.dot(x, w1_ref[...], preferred_element_type=F32) + b1_ref[...]
        glu = jnp.minimum(u[:, :D_EXPERT], SWIGLU_LIMIT)
        lin = jnp.clip(u[:, D_EXPERT:], -SWIGLU_LIMIT, SWIGLU_LIMIT)
        a = glu * jax.nn.sigmoid(SWIGLU_ALPHA * glu) * (lin + 1.0)
        y = jnp.dot(a.astype(BF16), w2b[...], preferred_element_type=F32) + b2_ref[...]
        for c in range(chunks):
            y_ref[pl.ds(c, tm, stride=chunks), :] = y[:, c * LANES:(c + 1) * LANES]

    @pl.when(tvalid_ref[i] == 0)
    def _():
        y_ref[...] = jnp.zeros(y_ref.shape, F32)


def _experts(te, tfirst, tvalid, txb, xs_lin, w1p, b1p, w2, b2, layer, d, tm):
    n_tiles = te.shape[0]
    chunks = d // LANES
    e0 = layer * N_EXPERTS
    return pl.pallas_call(
        functools.partial(_expert_kernel, tm=tm, chunks=chunks),
        grid_spec=pltpu.PrefetchScalarGridSpec(
            num_scalar_prefetch=4,
            grid=(n_tiles,),
            in_specs=[
                pl.BlockSpec((tm * chunks, LANES), lambda i, te, tf, tv, txb: (txb[i], 0)),
                pl.BlockSpec((None, d, 2 * D_EXPERT), lambda i, te, *_: (e0 + te[i], 0, 0)),
                pl.BlockSpec((None, 1, 2 * D_EXPERT), lambda i, te, *_: (e0 + te[i], 0, 0)),
                pl.BlockSpec((None, D_EXPERT, d), lambda i, te, *_: (e0 + te[i], 0, 0)),
                pl.BlockSpec((None, 1, d), lambda i, te, *_: (e0 + te[i], 0, 0)),
            ],
            out_specs=pl.BlockSpec((tm * chunks, LANES), lambda i, *_: (i, 0)),
            scratch_shapes=[pltpu.VMEM((D_EXPERT, d), BF16)],
        ),
        out_shape=jax.ShapeDtypeStruct((n_tiles * tm * chunks, LANES), F32),
        compiler_params=_cparams(("arbitrary",)),
        name="experts",
    )(te, tfirst, tvalid, txb, xs_lin, w1p, b1p, w2, b2)


def _combine_kernel(pos_ref, ys_hbm, tw_ref, x_ref, g2_ref, lg_ref, lb_ref,
                    o_ref, gbuf, sem, *, alpha, tm, chunks):
    i = pl.program_id(0)
    n = pl.num_programs(0)
    slot = i % 2

    def issue(step, slot_):
        def token(r, c):
            dst_rows = pl.ds(pl.multiple_of(r * chunks, chunks), chunks)
            base = (step * tm + r) * TOP_K
            for k in range(TOP_K):
                s = pl.multiple_of(pos_ref[base + k] * chunks, chunks)
                pltpu.make_async_copy(ys_hbm.at[pl.ds(s, chunks), :],
                                      gbuf.at[slot_, k, dst_rows, :], sem.at[slot_]).start()
            return c
        lax.fori_loop(0, tm, token, 0, unroll=2)

    @pl.when(i == 0)
    def _():
        issue(0, 0)

    @pl.when(i + 1 < n)
    def _():
        issue(i + 1, 1 - slot)

    for k in range(TOP_K):
        pltpu.make_async_copy(ys_hbm.at[pl.ds(0, tm * chunks), :], gbuf.at[slot, k],
                              sem.at[slot]).wait()

    tw = tw_ref[...]
    gate = [jnp.broadcast_to(tw[:, k:k + 1], (tm, LANES)) for k in range(TOP_K)]
    cols = []
    for c in range(chunks):
        sl = pl.ds(c, tm, stride=chunks)
        acc = gate[0] * gbuf[slot, 0, sl, :]
        for k in range(1, TOP_K):
            acc += gate[k] * gbuf[slot, k, sl, :]
        cols.append(acc)
    y = jnp.concatenate(cols, axis=1)
    o_ref[...] = _ln(alpha * x_ref[...] + g2_ref[...] * y) * lg_ref[...] + lb_ref[...]


def _combine(pos, ys_lin, top_w, x1, modl, lg, lb, n_rows, n_lat_rows, tm, alpha):
    d = x1.shape[1]
    chunks = d // LANES
    n_lat_tiles = n_lat_rows // tm
    which = lambda i: jnp.where(i >= n_lat_tiles, 1, 0)
    const2 = lambda a: pl.BlockSpec(a.shape, lambda i, *_: (0, 0))
    return pl.pallas_call(
        functools.partial(_combine_kernel, alpha=alpha, tm=tm, chunks=chunks),
        grid_spec=pltpu.PrefetchScalarGridSpec(
            num_scalar_prefetch=1,
            grid=(n_rows // tm,),
            in_specs=[pl.BlockSpec(memory_space=pl.ANY),
                      pl.BlockSpec((tm, LANES), lambda i, *_: (i, 0)),
                      pl.BlockSpec((tm, d), lambda i, *_: (i, 0)),
                      pl.BlockSpec((None, 1, d), lambda i, *_: (which(i) * 6 + 5, 0, 0)),
                      const2(lg), const2(lb)],
            out_specs=pl.BlockSpec((tm, d), lambda i, *_: (i, 0)),
            scratch_shapes=[pltpu.VMEM((2, TOP_K, tm * chunks, LANES), F32),
                            pltpu.SemaphoreType.DMA((2,))],
        ),
        out_shape=jax.ShapeDtypeStruct((n_rows, d), F32),
        compiler_params=_cparams(("arbitrary",)),
        name="combine",
    )(pos, ys_lin, top_w, x1, modl, lg, lb)


def _rope_tables(s_len, n_ctx):
    rows = s_len // GRID_W
    axis_dim = HEAD_DIM // 2
    row = jnp.repeat(jnp.arange(rows, dtype=F32), GRID_W)
    col = jnp.tile(jnp.arange(GRID_W, dtype=F32), rows)
    inv = ROPE_THETA ** (-jnp.arange(0, axis_dim, 2, dtype=F32) / axis_dim)
    ar = row[:, None] * inv[None, :]
    ac = col[:, None] * inv[None, :]
    ang = jnp.concatenate([ar, ar, ac, ac], -1)
    cos = jnp.tile(jnp.cos(ang), (1, 2))
    sin = jnp.tile(jnp.sin(ang), (1, 2))
    lo = (jnp.arange(LANES) % (axis_dim) < axis_dim // 2)[None, :]
    sa = jnp.where(lo, -sin, 0.0)
    sb = jnp.where(lo, 0.0, sin)
    pad = lambda t, v: jnp.concatenate([t, jnp.full((n_ctx, LANES), v, F32)], 0)
    return pad(cos, 1.0), pad(sa, 0.0), pad(sb, 0.0)


def kernel(x, c, ctx, c_ctx, w_ada, b_ada, w_in, q_gain, k_gain, sink, out_gain, w_out,
           ln1_g, ln1_b, w_router, b_router, w1, b1, w2, b2, ln2_g, ln2_b):
    bsz, s_len, d = x.shape
    n_ctx = ctx.shape[1]
    depth = w_ada.shape[0]
    assert bsz == 1 and d % LANES == 0
    nt = s_len + n_ctx
    alpha = (2.0 * depth) ** 0.25
    tm = 256

    condT = jnp.stack([c[0], c_ctx], axis=1)
    mod = _modulation(condT, w_ada, b_ada).reshape(depth, 12, 1, d)
    cos_t, sa_t, sb_t = _rope_tables(s_len, n_ctx)
    hid = jnp.arange(A_Q) // HEAD_DIM
    bd = (hid[:, None] == hid[None, :]).astype(BF16)
    w_in_b = w_in.astype(BF16)
    w_out_b = w_out.astype(BF16)
    n_e = w1.shape[1]
    w1p = _w1_prep(w1)
    b1p = jnp.concatenate([b1[..., 0::2], b1[..., 1::2]], axis=-1).reshape(depth * n_e, 1, -1)
    w2r = w2.reshape(depth * n_e, D_EXPERT, d)
    b2r = b2.reshape(depth * n_e, 1, d)
    wr_p = jnp.pad(w_router, ((0, 0), (0, 0), (0, LANES - N_EXPERTS)))
    wr_hi = wr_p.astype(BF16)
    wr_lo = (wr_p - wr_hi.astype(F32)).astype(BF16)
    br_p = jnp.pad(b_router, ((0, 0), (0, LANES - N_EXPERTS)), constant_values=NEG_BIG)

    x_all = jnp.concatenate([x[0], ctx[0]], axis=0)
    for l in range(depth):
        last = l == depth - 1
        n_rows = s_len if last else nt
        modl = mod[l]
        gq_t = jnp.tile(q_gain[l], N_HEADS)[None, :]
        gk_t = jnp.tile(k_gain[l], KV_HEADS)[None, :]
        qa, kaT, va, qb, kbT, vb, ksq = _project(x_all, modl, w_in_b, l, gq_t, gk_t, bd,
                                                 cos_t, sa_t, sb_t, s_len, tm)
        o_a = _attn_a(sink[l], qa, kaT, va, n_rows, s_len)
        kmax = jnp.sqrt(jnp.max(ksq[:, :KV_HEADS], axis=0))
        o_b = _attn_b(kmax, qb, kbT, vb, n_rows, s_len)
        x1, h2_lin, top_i, top_w, cnt = _merge(
            o_a, o_b, x_all, modl, out_gain[l][None, :], w_out_b, l, ln1_g[l][None, :],
            ln1_b[l][None, :], wr_hi, wr_lo, br_p[l][None, :], n_rows, s_len, tm, alpha)
        pstart, te, tfirst, tvalid, txb, tzero = _tile_meta(cnt, n_rows * TOP_K, tm)
        eid = top_i[:, :TOP_K]
        of_expert = eid[:, :, None] == jnp.arange(N_EXPERTS, dtype=jnp.int32)[None, None, :]
        pos = (jnp.sum(jnp.where(of_expert, pstart[None, None, :], 0), axis=-1)
               + top_i[:, TOP_K:2 * TOP_K]).reshape(-1)
        xs_lin = _dispatch(pos, tzero, h2_lin, n_rows, d, tm)
        ys_lin = _experts(te, tfirst, tvalid, txb, xs_lin, w1p, b1p, w2r, b2r, l, d, tm)
        x_all = _combine(pos, ys_lin, top_w, x1, modl, ln2_g[l][None, :],
                         ln2_b[l][None, :], n_rows, s_len, tm, alpha)
    return x_all[None]
```

```python
import functools

import jax
import jax.numpy as jnp
from jax import lax
from jax.experimental import pallas as pl
from jax.experimental.pallas import tpu as pltpu

F32 = jnp.float32
BF16 = jnp.bfloat16

GRID_W = 64
HEAD_DIM = 64
N_HEADS = 16
KV_HEADS = 2
GROUP = N_HEADS // KV_HEADS
WINDOW = 128
BLOCK = 128
ROPE_THETA = 10000.0
ATTN_SCALE = HEAD_DIM ** -0.5
A_Q = N_HEADS * HEAD_DIM
A_KV = KV_HEADS * HEAD_DIM
N_EXPERTS = 32
TOP_K = 4
D_EXPERT = 512
SWIGLU_LIMIT = 7.0
SWIGLU_ALPHA = 1.702
LN_EPS = 1e-5
RMS_EPS = 1e-6
LOG2_E = 1.4426950408889634
LANES = 128
NEG_BIG = -1e30

VMEM_LIMIT = 56 * 1024 * 1024


def _cparams(sem):
    return pltpu.CompilerParams(dimension_semantics=sem, vmem_limit_bytes=VMEM_LIMIT)


def _ln(x):
    mu = jnp.mean(x, -1, keepdims=True)
    xc = x - mu
    var = jnp.mean(xc * xc, -1, keepdims=True)
    return xc * lax.rsqrt(var + LN_EPS)


def _mod_kernel(condT_ref, w_ref, b_ref, o_ref):
    ct = condT_ref[...]
    s = ct * jax.nn.sigmoid(ct)
    w = w_ref[...]
    r0 = jnp.sum(s[:, 0:1] * w, axis=0, keepdims=True)
    r1 = jnp.sum(s[:, 1:2] * w, axis=0, keepdims=True)
    o_ref[...] = jnp.concatenate([r0, r1], axis=0) + b_ref[...]


def _modulation(condT, w_ada, b_ada):
    depth, d, e = w_ada.shape
    tn = 512
    return pl.pallas_call(
        _mod_kernel,
        grid=(depth, e // tn),
        in_specs=[
            pl.BlockSpec((d, 2), lambda l, j: (0, 0)),
            pl.BlockSpec((None, d, tn), lambda l, j: (l, 0, j)),
            pl.BlockSpec((None, 1, tn), lambda l, j: (l, 0, j)),
        ],
        out_specs=pl.BlockSpec((None, 2, tn), lambda l, j: (l, 0, j)),
        out_shape=jax.ShapeDtypeStruct((depth, 2, e), F32),
        compiler_params=_cparams(("parallel", "parallel")),
        name="modulation",
    )(condT, w_ada, b_ada.reshape(depth, 1, e))


def _proj_kernel(x_ref, sh_ref, sc_ref, w_ref, gq_ref, gk_ref, bd_ref,
                 cos_ref, sa_ref, sb_ref,
                 qa_ref, kaT_ref, va_ref, qb_ref, kbT_ref, vb_ref, ksq_ref):
    x = x_ref[...]
    h = _ln(x) * (1.0 + sc_ref[...]) + sh_ref[...]
    p = jnp.dot(h.astype(BF16), w_ref[...], preferred_element_type=F32)
    cos, sa, sb = cos_ref[...], sa_ref[...], sb_ref[...]
    lane = lax.broadcasted_iota(jnp.int32, cos.shape, 1)

    def rope(t):
        return (t * cos + pltpu.roll(t, LANES - 16, 1) * sa
                + pltpu.roll(t, 16, 1) * sb)

    def head_rms(t, width):
        sq = t * t
        hi = sq.astype(BF16)
        lo = (sq - hi.astype(F32)).astype(BF16)
        bd = bd_ref[0:width, 0:width]
        ss = (jnp.dot(hi, bd, preferred_element_type=F32)
              + jnp.dot(lo, bd, preferred_element_type=F32))
        return lax.rsqrt(ss * (1.0 / HEAD_DIM) + RMS_EPS)

    def with_ones(t, g):
        src = t if g == 0 else pltpu.roll(t, HEAD_DIM, 1)
        return jnp.where(lane < HEAD_DIM, src,
                         jnp.where(lane == HEAD_DIM, 1.0, 0.0)).astype(BF16)

    def write_q(q_out, q, scale):
        for c in range(A_Q // LANES):
            r = rope(q[:, c * LANES:(c + 1) * LANES]) * scale
            q_out[2 * c] = r[:, :HEAD_DIM].astype(BF16)
            q_out[2 * c + 1] = r[:, HEAD_DIM:].astype(BF16)

    o = 0
    write_q(qa_ref, p[:, o:o + A_Q], ATTN_SCALE); o += A_Q
    kaT_ref[...] = rope(p[:, o:o + A_KV]).T.astype(BF16); o += A_KV
    va = p[:, o:o + A_KV]; o += A_KV
    va_ref[0] = with_ones(va, 0)
    va_ref[1] = with_ones(va, 1)
    qb = p[:, o:o + A_Q]; o += A_Q
    write_q(qb_ref, qb * head_rms(qb, A_Q) * gq_ref[...], ATTN_SCALE * LOG2_E)
    kb = p[:, o:o + A_KV]; o += A_KV
    kb = rope(kb * head_rms(kb, A_KV) * gk_ref[...])
    kbT_ref[...] = kb.T.astype(BF16)
    kq = kb.astype(BF16).astype(F32)
    ksq = kq * kq
    n0 = jnp.sum(jnp.where(lane < HEAD_DIM, ksq, 0.0), axis=1, keepdims=True)
    n1 = jnp.sum(jnp.where(lane >= HEAD_DIM, ksq, 0.0), axis=1, keepdims=True)
    ksq_ref[...] = jnp.where(lane == 0, n0, jnp.where(lane == 1, n1, 0.0))
    vb = p[:, o:o + A_KV]
    vb_ref[0] = with_ones(vb, 0)
    vb_ref[1] = with_ones(vb, 1)


def _project(x_all, modl, w_in_b, layer, gq_t, gk_t, bd, cos_t, sa_t, sb_t, n_lat_rows, tm):
    nt, d = x_all.shape
    n_lat_tiles = n_lat_rows // tm
    which = lambda i: jnp.where(i >= n_lat_tiles, 1, 0)
    modspec = lambda j: pl.BlockSpec((None, 1, d), lambda i: (which(i) * 6 + j, 0, 0))
    const2 = lambda a: pl.BlockSpec(a.shape, lambda i: (0, 0))
    layer3 = lambda a: pl.BlockSpec((None,) + a.shape[1:], lambda i: (layer, 0, 0))
    q_shape = jax.ShapeDtypeStruct((N_HEADS, nt, HEAD_DIM), BF16)
    kT_shape = jax.ShapeDtypeStruct((A_KV, nt), BF16)
    v_shape = jax.ShapeDtypeStruct((KV_HEADS, nt, LANES), BF16)
    q_spec = pl.BlockSpec((N_HEADS, tm, HEAD_DIM), lambda i: (0, i, 0))
    kT_spec = pl.BlockSpec((A_KV, tm), lambda i: (0, i))
    v_spec = pl.BlockSpec((KV_HEADS, tm, LANES), lambda i: (0, i, 0))
    tab_spec = pl.BlockSpec((tm, LANES), lambda i: (i, 0))
    return pl.pallas_call(
        _proj_kernel,
        grid=(nt // tm,),
        in_specs=[pl.BlockSpec((tm, d), lambda i: (i, 0)), modspec(0), modspec(1),
                  layer3(w_in_b), const2(gq_t), const2(gk_t), const2(bd),
                  tab_spec, tab_spec, tab_spec],
        out_specs=[q_spec, kT_spec, v_spec, q_spec, kT_spec, v_spec, tab_spec],
        out_shape=[q_shape, kT_shape, v_shape, q_shape, kT_shape, v_shape,
                   jax.ShapeDtypeStruct((nt, LANES), F32)],
        compiler_params=_cparams(("parallel",)),
        name="project",
    )(x_all, modl, modl, w_in_b, gq_t, gk_t, bd, cos_t, sa_t, sb_t)


FIXED_REF_LIMIT = 50.0


def _attn_b_kernel(kmax_ref, q_ref, kT_ref, v_ref, o_ref, m_scr, acc_scr,
                   *, n_lat_rows, tk, blocks_per_step):
    i = pl.program_id(0)
    tq = q_ref.shape[1]
    nt = kT_ref.shape[1]
    is_ctx = i * tq >= n_lat_rows
    acc_scr[...] = jnp.zeros(acc_scr.shape, F32)

    bound = jnp.zeros((tq, LANES), F32)
    for h in range(N_HEADS):
        qf = q_ref[h].astype(F32)
        qn = jnp.sqrt(jnp.sum(qf * qf, axis=1, keepdims=True))
        m_h = jnp.broadcast_to(qn * kmax_ref[h // GROUP], (tq, LANES))
        m_scr[h] = m_h
        bound = jnp.maximum(bound, m_h)
    fixed_ref_ok = jnp.max(bound) <= FIXED_REF_LIMIT

    def keys_fixed(off, width):
        for h in range(N_HEADS):
            g = h // GROUP
            kt = kT_ref[g * HEAD_DIM:(g + 1) * HEAD_DIM, pl.ds(off, width)]
            s = jnp.dot(q_ref[h], kt, preferred_element_type=F32)
            p = jnp.exp2(s - jnp.concatenate([m_scr[h]] * (width // LANES), axis=1))
            acc_scr[h] += jnp.dot(p.astype(BF16), v_ref[g, pl.ds(off, width), :],
                                  preferred_element_type=F32)

    def sweep(update, per_step):
        def step(j, carry):
            for u in range(per_step):
                update(pl.multiple_of(j * (tk * per_step) + u * tk, tk), tk)
            return carry

        lax.fori_loop(0, jnp.where(is_ctx, 0, n_lat_rows // (tk * per_step)), step, 0)
        update(n_lat_rows, nt - n_lat_rows)

    @pl.when(fixed_ref_ok)
    def _():
        per_all = blocks_per_step[2]
        if per_all:
            @pl.when(is_ctx)
            def _():
                keys_fixed(n_lat_rows, tk)

            @pl.when(jnp.logical_not(is_ctx))
            def _():
                def step(j, carry):
                    for u in range(per_all):
                        keys_fixed(pl.multiple_of(j * (tk * per_all) + u * tk, tk), tk)
                    return carry

                lax.fori_loop(0, nt // (tk * per_all), step, 0)
        else:
            sweep(keys_fixed, blocks_per_step[0])

    def keys(off, width):
        for h in range(N_HEADS):
            g = h // GROUP
            kt = kT_ref[g * HEAD_DIM:(g + 1) * HEAD_DIM, pl.ds(off, width)]
            s = jnp.dot(q_ref[h], kt, preferred_element_type=F32)
            m_prev = m_scr[h]
            m_new = jnp.maximum(m_prev, jnp.max(s, axis=1, keepdims=True))
            alpha = jnp.exp2(m_prev - m_new)
            p = jnp.exp2(s - jnp.concatenate([m_new] * (width // LANES), axis=1))
            pv = jnp.dot(p.astype(BF16), v_ref[g, pl.ds(off, width), :],
                         preferred_element_type=F32)
            acc_scr[h] = acc_scr[h] * alpha + pv
            m_scr[h] = m_new

    @pl.when(jnp.logical_not(fixed_ref_ok))
    def _():
        m_scr[...] = jnp.full(m_scr.shape, NEG_BIG, F32)
        sweep(keys, blocks_per_step[1])

    for h in range(N_HEADS):
        acc = acc_scr[h]
        o_ref[:, h * HEAD_DIM:(h + 1) * HEAD_DIM] = (
            acc[:, :HEAD_DIM] / acc[:, HEAD_DIM:HEAD_DIM + 1])


def _attn_b(kmax, q_hm, kT, v2, n_q_rows, n_lat_rows, tq=256, tk=256):
    nt = kT.shape[1]
    n_blocks = n_lat_rows // tk
    per_all = 0
    if nt - n_lat_rows == tk:
        per_all = max(b for b in range(1, 13) if (n_blocks + 1) % b == 0)
        per_all = per_all if per_all >= 3 else 0
    assert n_lat_rows % tk == 0 and n_blocks % 2 == 0
    blocks_per_step = (8 if n_blocks % 8 == 0 else 2, 2, per_all)
    return pl.pallas_call(
        functools.partial(_attn_b_kernel, n_lat_rows=n_lat_rows, tk=tk,
                          blocks_per_step=blocks_per_step),
        grid_spec=pltpu.PrefetchScalarGridSpec(
            num_scalar_prefetch=1,
            grid=(n_q_rows // tq,),
            in_specs=[pl.BlockSpec((N_HEADS, tq, HEAD_DIM), lambda i, s: (0, i, 0)),
                      pl.BlockSpec((A_KV, nt), lambda i, s: (0, 0)),
                      pl.BlockSpec((KV_HEADS, nt, LANES), lambda i, s: (0, 0, 0))],
            out_specs=pl.BlockSpec((tq, A_Q), lambda i, s: (i, 0)),
            scratch_shapes=[pltpu.VMEM((N_HEADS, tq, LANES), F32),
                            pltpu.VMEM((N_HEADS, tq, LANES), F32)],
        ),
        out_shape=jax.ShapeDtypeStruct((n_q_rows, A_Q), F32),
        compiler_params=_cparams(("parallel",)),
        name="attn_global",
    )(kmax, q_hm, kT, v2)


def _attn_a_kernel(sink_ref, q_ref, kT_ref, v_ref, o_ref, *, n_lat_rows):
    n = pl.program_id(0)
    n_lat_blocks = n_lat_rows // BLOCK
    nt = kT_ref.shape[1]
    is_lat = n < n_lat_blocks
    ii = lax.broadcasted_iota(jnp.int32, (BLOCK, BLOCK), 0)
    jj = lax.broadcasted_iota(jnp.int32, (BLOCK, BLOCK), 1)
    blocks = []
    for d in (-1, 0, 1):
        blk = n + d
        valid = is_lat & (blk >= 0) & (blk < n_lat_blocks)
        off = pl.multiple_of(jnp.clip(blk, 0, n_lat_blocks - 1) * BLOCK, BLOCK)
        rel_ok = (jj >= ii) if d == -1 else ((jj <= ii) if d == 1 else (jj >= 0))
        blocks.append((off, valid & rel_ok))

    for h in range(N_HEADS):
        g = h // GROUP
        q = q_ref[h]
        kg = kT_ref.at[g * HEAD_DIM:(g + 1) * HEAD_DIM, :]
        vg = v_ref.at[g]
        sink = sink_ref[h]
        s_list = []
        for off, mask in blocks:
            s = jnp.dot(q, kg[:, pl.ds(off, BLOCK)], preferred_element_type=F32)
            s_list.append(jnp.where(mask, s, NEG_BIG))
        s_ctx = jnp.dot(q, kg[:, n_lat_rows:nt], preferred_element_type=F32)
        col_max = s_list[0]
        for s in s_list[1:]:
            col_max = jnp.maximum(col_max, s)
        for c in range((nt - n_lat_rows) // BLOCK):
            col_max = jnp.maximum(col_max, s_ctx[:, c * BLOCK:(c + 1) * BLOCK])
        m = jnp.maximum(jnp.max(col_max, axis=1, keepdims=True), sink)
        acc = jnp.dot(jnp.exp(s_ctx - m).astype(BF16), vg[n_lat_rows:nt, :],
                      preferred_element_type=F32)
        for (off, _), s in zip(blocks, s_list):
            acc += jnp.dot(jnp.exp(s - m).astype(BF16), vg[pl.ds(off, BLOCK), :],
                           preferred_element_type=F32)
        denom = acc[:, HEAD_DIM:HEAD_DIM + 1] + jnp.exp(sink - m)
        o_ref[:, h * HEAD_DIM:(h + 1) * HEAD_DIM] = acc[:, :HEAD_DIM] / denom


def _attn_a(sink_l, q_hm, kT, v2, n_q_rows, n_lat_rows):
    nt = kT.shape[1]
    return pl.pallas_call(
        functools.partial(_attn_a_kernel, n_lat_rows=n_lat_rows),
        grid_spec=pltpu.PrefetchScalarGridSpec(
            num_scalar_prefetch=1,
            grid=(n_q_rows // BLOCK,),
            in_specs=[pl.BlockSpec((N_HEADS, BLOCK, HEAD_DIM), lambda i, s: (0, i, 0)),
                      pl.BlockSpec((A_KV, nt), lambda i, s: (0, 0)),
                      pl.BlockSpec((KV_HEADS, nt, LANES), lambda i, s: (0, 0, 0))],
            out_specs=pl.BlockSpec((BLOCK, A_Q), lambda i, s: (i, 0)),
        ),
        out_shape=jax.ShapeDtypeStruct((n_q_rows, A_Q), F32),
        compiler_params=_cparams(("parallel",)),
        name="attn_window",
    )(sink_l, q_hm, kT, v2)


def _w1_prep_kernel(w_ref, se_ref, so_ref, o_ref):
    wb = w_ref[...].astype(BF16)
    half = D_EXPERT // 2
    for b in range(2):
        blk = wb[:, 2 * half * b:2 * half * (b + 1)]
        o_ref[:, half * b:half * (b + 1)] = jnp.dot(
            blk, se_ref[...], preferred_element_type=F32).astype(BF16)
        o_ref[:, D_EXPERT + half * b:D_EXPERT + half * (b + 1)] = jnp.dot(
            blk, so_ref[...], preferred_element_type=F32).astype(BF16)


def _w1_prep(w1):
    depth, n_e, d, n2 = w1.shape
    j = jnp.arange(D_EXPERT)[:, None]
    m = jnp.arange(D_EXPERT // 2)[None, :]
    sel_even = (j == 2 * m).astype(BF16)
    sel_odd = (j == 2 * m + 1).astype(BF16)
    const2 = lambda a: pl.BlockSpec(a.shape, lambda i: (0, 0))
    out = pl.pallas_call(
        _w1_prep_kernel,
        grid=(depth * n_e,),
        in_specs=[pl.BlockSpec((None, d, n2), lambda i: (i, 0, 0)), const2(sel_even), const2(sel_odd)],
        out_specs=pl.BlockSpec((None, d, n2), lambda i: (i, 0, 0)),
        out_shape=jax.ShapeDtypeStruct((depth * n_e, d, n2), BF16),
        compiler_params=_cparams(("parallel",)),
        name="w1_prep",
    )(w1.reshape(depth * n_e, d, n2), sel_even, sel_odd)
    return out


def _merge_kernel(oa_ref, ob_ref, x_ref, og_ref, wout_ref, g1_ref, lg_ref, lb_ref,
                  sh2_ref, sc2_ref, wrh_ref, wrl_ref, br_ref,
                  x1_ref, h2_ref, ti_ref, tw_ref, cnt_ref, *, alpha):
    def rms(t, g):
        return t * lax.rsqrt(jnp.mean(t * t, -1, keepdims=True) + RMS_EPS) * g

    og = og_ref[...]
    o = jnp.concatenate([rms(oa_ref[...], og[:, :A_Q]), rms(ob_ref[...], og[:, A_Q:])], axis=1)
    mix = jnp.dot(o.astype(BF16), wout_ref[...], preferred_element_type=F32)
    x1 = _ln(alpha * x_ref[...] + g1_ref[...] * mix) * lg_ref[...] + lb_ref[...]
    x1_ref[...] = x1
    h2 = _ln(x1) * (1.0 + sc2_ref[...]) + sh2_ref[...]
    tm, d = h2.shape
    chunks = d // LANES
    for c in range(chunks):
        h2_ref[pl.ds(c, tm, stride=chunks), :] = h2[:, c * LANES:(c + 1) * LANES]

    h_hi = h2.astype(BF16)
    h_lo = (h2 - h_hi.astype(F32)).astype(BF16)
    w_hi = wrh_ref[...]
    logits = ((jnp.dot(h_hi, w_hi, preferred_element_type=F32)
               + jnp.dot(h_lo, w_hi, preferred_element_type=F32))
              + jnp.dot(h_hi, wrl_ref[...], preferred_element_type=F32)) + br_ref[...]
    lane = lax.broadcasted_iota(jnp.int32, logits.shape, 1)
    lane_f = lane.astype(F32)
    work = logits
    vals, ids = [], []
    for _ in range(TOP_K):
        mx = jnp.max(work, axis=1, keepdims=True)
        idx = jnp.min(jnp.where(work == mx, lane_f, float(LANES)), axis=1, keepdims=True)
        vals.append(mx)
        ids.append(idx)
        work = jnp.where(lane_f == idx, NEG_BIG, work)
    es = [jnp.exp(v - vals[0]) for v in vals]
    tot = es[0] + es[1] + es[2] + es[3]

    @pl.when(pl.program_id(0) == 0)
    def _():
        cnt_ref[...] = jnp.zeros(cnt_ref.shape, F32)

    onehot = [(lane_f == ids[k]).astype(F32) for k in range(TOP_K)]
    picked = (onehot[0] + onehot[1]) + (onehot[2] + onehot[3])
    earlier = (lax.broadcasted_iota(jnp.int32, (tm, tm), 1)
               < lax.broadcasted_iota(jnp.int32, (tm, tm), 0)).astype(BF16)
    before = jnp.dot(earlier, picked.astype(BF16), preferred_element_type=F32) + cnt_ref[...]
    ranks = [jnp.sum(onehot[k] * before, axis=1, keepdims=True) for k in range(TOP_K)]
    cnt_ref[...] += jnp.sum(picked, axis=0, keepdims=True)

    ti = jnp.zeros(logits.shape, jnp.int32)
    tw = jnp.zeros(logits.shape, F32)
    for k in range(TOP_K):
        ti = jnp.where(lane == k, ids[k].astype(jnp.int32), ti)
        ti = jnp.where(lane == TOP_K + k, ranks[k].astype(jnp.int32), ti)
        tw = jnp.where(lane == k, es[k] / tot, tw)
    ti_ref[...] = ti
    tw_ref[...] = tw


def _merge(o_a, o_b, x_all, modl, og, w_out_b, layer, lg, lb, wr_hi, wr_lo, br_p, n_rows,
           n_lat_rows, tm, alpha):
    d = x_all.shape[1]
    chunks = d // LANES
    n_lat_tiles = n_lat_rows // tm
    which = lambda i: jnp.where(i >= n_lat_tiles, 1, 0)
    modspec = lambda j: pl.BlockSpec((None, 1, d), lambda i: (which(i) * 6 + j, 0, 0))
    const2 = lambda a: pl.BlockSpec(a.shape, lambda i: (0, 0))
    layer3 = lambda a: pl.BlockSpec((None,) + a.shape[1:], lambda i: (layer, 0, 0))
    row = lambda w: pl.BlockSpec((tm, w), lambda i: (i, 0))
    return pl.pallas_call(
        functools.partial(_merge_kernel, alpha=alpha),
        grid=(n_rows // tm,),
        in_specs=[row(A_Q), row(A_Q), row(d), const2(og), layer3(w_out_b), modspec(2),
                  const2(lg), const2(lb), modspec(3), modspec(4), layer3(wr_hi), layer3(wr_lo),
                  const2(br_p)],
        out_specs=[row(d), pl.BlockSpec((tm * chunks, LANES), lambda i: (i, 0)),
                   row(LANES), row(LANES), pl.BlockSpec((1, LANES), lambda i: (0, 0))],
        out_shape=[jax.ShapeDtypeStruct((n_rows, d), F32),
                   jax.ShapeDtypeStruct((n_rows * chunks, LANES), F32),
                   jax.ShapeDtypeStruct((n_rows, LANES), jnp.int32),
                   jax.ShapeDtypeStruct((n_rows, LANES), F32),
                   jax.ShapeDtypeStruct((1, LANES), F32)],
        compiler_params=_cparams(("arbitrary",)),
        name="merge_route",
    )(o_a, o_b, x_all, og, w_out_b, modl, lg, lb, modl, modl, wr_hi, wr_lo, br_p)


def _tile_meta(cnt, n_flat, tm):
    i32 = jnp.int32
    counts = cnt[0, :N_EXPERTS].astype(i32)
    n_tiles = n_flat // tm + N_EXPERTS
    tiles_per = (counts + tm - 1) // tm
    tile_end = jnp.cumsum(tiles_per).astype(i32)
    tile_start = tile_end - tiles_per
    tid = jnp.arange(n_tiles, dtype=i32)
    used = tid < tile_end[-1]
    te = jnp.minimum(jnp.sum((tile_end[None, :] <= tid[:, None]).astype(i32), axis=1), N_EXPERTS - 1)
    onehot = te[:, None] == jnp.arange(N_EXPERTS, dtype=i32)[None, :]
    cnt_t = jnp.sum(jnp.where(onehot, counts[None, :], 0), axis=1)
    start_t = jnp.sum(jnp.where(onehot, tile_start[None, :], 0), axis=1)
    tvalid = jnp.where(used, jnp.clip(cnt_t - (tid - start_t) * tm, 0, tm), 0).astype(i32)
    tfirst = (used & (tid == start_t)).astype(i32)
    txb = jnp.where(used, tid, tile_end[-1] - 1).astype(i32)
    tzero = (tvalid < tm).astype(i32)
    return (tile_start * tm).astype(i32), te.astype(i32), tfirst, tvalid, txb, tzero


def _dispatch_kernel(pos_ref, tzero_ref, h2_ref, xs_hbm, zbuf, sem, zsem,
                     *, tb, tm, chunks, n_tiles):
    i = pl.program_id(0)

    @pl.when(i == 0)
    def _():
        zbuf[...] = jnp.zeros(zbuf.shape, F32)

        def fill(t):
            off = pl.multiple_of(t * (tm * chunks), tm * chunks)
            return pltpu.make_async_copy(zbuf, xs_hbm.at[pl.ds(off, tm * chunks), :], zsem)

        def start(t, c):
            @pl.when(tzero_ref[t] > 0)
            def _():
                fill(t).start()
            return c

        def wait(t, c):
            @pl.when(tzero_ref[t] > 0)
            def _():
                fill(t).wait()
            return c

        lax.fori_loop(0, n_tiles, start, 0)
        lax.fori_loop(0, n_tiles, wait, 0)

    def start_token(r, c):
        src = h2_ref.at[pl.ds(pl.multiple_of(r * chunks, chunks), chunks), :]
        base = (i * tb + r) * TOP_K
        for k in range(TOP_K):
            t = pl.multiple_of(pos_ref[base + k] * chunks, chunks)
            pltpu.make_async_copy(src, xs_hbm.at[pl.ds(t, chunks), :], sem).start()
        return c

    lax.fori_loop(0, tb, start_token, 0, unroll=2)
    for _ in range(TOP_K):
        pltpu.make_async_copy(h2_ref, xs_hbm.at[pl.ds(0, tb * chunks), :], sem).wait()


def _dispatch(pos, tzero, h2_lin, n_tok, d, tm, tb=256):
    chunks = d // LANES
    n_tiles = tzero.shape[0]
    return pl.pallas_call(
        functools.partial(_dispatch_kernel, tb=tb, tm=tm, chunks=chunks, n_tiles=n_tiles),
        grid_spec=pltpu.PrefetchScalarGridSpec(
            num_scalar_prefetch=2,
            grid=(n_tok // tb,),
            in_specs=[pl.BlockSpec((tb * chunks, LANES), lambda i, *_: (i, 0))],
            out_specs=pl.BlockSpec(memory_space=pl.ANY),
            scratch_shapes=[pltpu.VMEM((tm * chunks, LANES), F32),
                            pltpu.SemaphoreType.DMA, pltpu.SemaphoreType.DMA],
        ),
        out_shape=jax.ShapeDtypeStruct((n_tiles * tm * chunks, LANES), F32),
        compiler_params=_cparams(("arbitrary",)),
        name="dispatch",
    )(pos, tzero, h2_lin)


def _expert_kernel(te_ref, tfirst_ref, tvalid_ref, txb_ref,
                   x_ref, w1_ref, b1_ref, w2_ref, b2_ref, y_ref, w2b, *, tm, chunks):
    i = pl.program_id(0)

    @pl.when(tvalid_ref[i] > 0)
    def _():
        @pl.when(tfirst_ref[i] > 0)
        def _():
            w2b[...] = w2_ref[...].astype(BF16)

        x = jnp.concatenate([x_ref[pl.ds(c, tm, stride=chunks), :] for c in range(chunks)],
                            axis=1).astype(BF16)
        u = jnp.dot(x, w1_ref[...], preferred_element_type=F32) + b1_ref[...]
        glu = jnp.minimum(u[:, :D_EXPERT], SWIGLU_LIMIT)
        lin = jnp.clip(u[:, D_EXPERT:], -SWIGLU_LIMIT, SWIGLU_LIMIT)
        a = glu * jax.nn.sigmoid(SWIGLU_ALPHA * glu) * (lin + 1.0)
        y = jnp.dot(a.astype(BF16), w2b[...], preferred_element_type=F32) + b2_ref[...]
        for c in range(chunks):
            y_ref[pl.ds(c, tm, stride=chunks), :] = y[:, c * LANES:(c + 1) * LANES]

    @pl.when(tvalid_ref[i] == 0)
    def _():
        y_ref[...] = jnp.zeros(y_ref.shape, F32)


def _experts(te, tfirst, tvalid, txb, xs_lin, w1p, b1p, w2, b2, layer, d, tm):
    n_tiles = te.shape[0]
    chunks = d // LANES
    e0 = layer * N_EXPERTS
    return pl.pallas_call(
        functools.partial(_expert_kernel, tm=tm, chunks=chunks),
        grid_spec=pltpu.PrefetchScalarGridSpec(
            num_scalar_prefetch=4,
            grid=(n_tiles,),
            in_specs=[
                pl.BlockSpec((tm * chunks, LANES), lambda i, te, tf, tv, txb: (txb[i], 0)),
                pl.BlockSpec((None, d, 2 * D_EXPERT), lambda i, te, *_: (e0 + te[i], 0, 0)),
                pl.BlockSpec((None, 1, 2 * D_EXPERT), lambda i, te, *_: (e0 + te[i], 0, 0)),
                pl.BlockSpec((None, D_EXPERT, d), lambda i, te, *_: (e0 + te[i], 0, 0)),
                pl.BlockSpec((None, 1, d), lambda i, te, *_: (e0 + te[i], 0, 0)),
            ],
            out_specs=pl.BlockSpec((tm * chunks, LANES), lambda i, *_: (i, 0)),
            scratch_shapes=[pltpu.VMEM((D_EXPERT, d), BF16)],
        ),
        out_shape=jax.ShapeDtypeStruct((n_tiles * tm * chunks, LANES), F32),
        compiler_params=_cparams(("arbitrary",)),
        name="experts",
    )(te, tfirst, tvalid, txb, xs_lin, w1p, b1p, w2, b2)


def _combine_kernel(pos_ref, ys_hbm, tw_ref, x_ref, g2_ref, lg_ref, lb_ref,
                    o_ref, gbuf, sem, *, alpha, tm, chunks):
    i = pl.program_id(0)
    n = pl.num_programs(0)
    slot = i % 2

    def issue(step, slot_):
        def token(r, c):
            dst_rows = pl.ds(pl.multiple_of(r * chunks, chunks), chunks)
            base = (step * tm + r) * TOP_K
            for k in range(TOP_K):
                s = pl.multiple_of(pos_ref[base + k] * chunks, chunks)
                pltpu.make_async_copy(ys_hbm.at[pl.ds(s, chunks), :],
                                      gbuf.at[slot_, k, dst_rows, :], sem.at[slot_]).start()
            return c
        lax.fori_loop(0, tm, token, 0, unroll=2)

    @pl.when(i == 0)
    def _():
        issue(0, 0)

    @pl.when(i + 1 < n)
    def _():
        issue(i + 1, 1 - slot)

    for k in range(TOP_K):
        pltpu.make_async_copy(ys_hbm.at[pl.ds(0, tm * chunks), :], gbuf.at[slot, k],
                              sem.at[slot]).wait()

    tw = tw_ref[...]
    gate = [jnp.broadcast_to(tw[:, k:k + 1], (tm, LANES)) for k in range(TOP_K)]
    cols = []
    for c in range(chunks):
        sl = pl.ds(c, tm, stride=chunks)
        acc = gate[0] * gbuf[slot, 0, sl, :]
        for k in range(1, TOP_K):
            acc += gate[k] * gbuf[slot, k, sl, :]
        cols.append(acc)
    y = jnp.concatenate(cols, axis=1)
    o_ref[...] = _ln(alpha * x_ref[...] + g2_ref[...] * y) * lg_ref[...] + lb_ref[...]


def _combine(pos, ys_lin, top_w, x1, modl, lg, lb, n_rows, n_lat_rows, tm, alpha):
    d = x1.shape[1]
    chunks = d // LANES
    n_lat_tiles = n_lat_rows // tm
    which = lambda i: jnp.where(i >= n_lat_tiles, 1, 0)
    const2 = lambda a: pl.BlockSpec(a.shape, lambda i, *_: (0, 0))
    return pl.pallas_call(
        functools.partial(_combine_kernel, alpha=alpha, tm=tm, chunks=chunks),
        grid_spec=pltpu.PrefetchScalarGridSpec(
            num_scalar_prefetch=1,
            grid=(n_rows // tm,),
            in_specs=[pl.BlockSpec(memory_space=pl.ANY),
                      pl.BlockSpec((tm, LANES), lambda i, *_: (i, 0)),
                      pl.BlockSpec((tm, d), lambda i, *_: (i, 0)),
                      pl.BlockSpec((None, 1, d), lambda i, *_: (which(i) * 6 + 5, 0, 0)),
                      const2(lg), const2(lb)],
            out_specs=pl.BlockSpec((tm, d), lambda i, *_: (i, 0)),
            scratch_shapes=[pltpu.VMEM((2, TOP_K, tm * chunks, LANES), F32),
                            pltpu.SemaphoreType.DMA((2,))],
        ),
        out_shape=jax.ShapeDtypeStruct((n_rows, d), F32),
        compiler_params=_cparams(("arbitrary",)),
        name="combine",
    )(pos, ys_lin, top_w, x1, modl, lg, lb)


def _rope_tables(s_len, n_ctx):
    rows = s_len // GRID_W
    axis_dim = HEAD_DIM // 2
    row = jnp.repeat(jnp.arange(rows, dtype=F32), GRID_W)
    col = jnp.tile(jnp.arange(GRID_W, dtype=F32), rows)
    inv = ROPE_THETA ** (-jnp.arange(0, axis_dim, 2, dtype=F32) / axis_dim)
    ar = row[:, None] * inv[None, :]
    ac = col[:, None] * inv[None, :]
    ang = jnp.concatenate([ar, ar, ac, ac], -1)
    cos = jnp.tile(jnp.cos(ang), (1, 2))
    sin = jnp.tile(jnp.sin(ang), (1, 2))
    lo = (jnp.arange(LANES) % (axis_dim) < axis_dim // 2)[None, :]
    sa = jnp.where(lo, -sin, 0.0)
    sb = jnp.where(lo, 0.0, sin)
    pad = lambda t, v: jnp.concatenate([t, jnp.full((n_ctx, LANES), v, F32)], 0)
    return pad(cos, 1.0), pad(sa, 0.0), pad(sb, 0.0)


def kernel(x, c, ctx, c_ctx, w_ada, b_ada, w_in, q_gain, k_gain, sink, out_gain, w_out,
           ln1_g, ln1_b, w_router, b_router, w1, b1, w2, b2, ln2_g, ln2_b):
    bsz, s_len, d = x.shape
    n_ctx = ctx.shape[1]
    depth = w_ada.shape[0]
    assert bsz == 1 and d % LANES == 0
    nt = s_len + n_ctx
    alpha = (2.0 * depth) ** 0.25
    tm = 256

    condT = jnp.stack([c[0], c_ctx], axis=1)
    mod = _modulation(condT, w_ada, b_ada).reshape(depth, 12, 1, d)
    cos_t, sa_t, sb_t = _rope_tables(s_len, n_ctx)
    hid = jnp.arange(A_Q) // HEAD_DIM
    bd = (hid[:, None] == hid[None, :]).astype(BF16)
    w_in_b = w_in.astype(BF16)
    w_out_b = w_out.astype(BF16)
    n_e = w1.shape[1]
    w1p = _w1_prep(w1)
    b1p = jnp.concatenate([b1[..., 0::2], b1[..., 1::2]], axis=-1).reshape(depth * n_e, 1, -1)
    w2r = w2.reshape(depth * n_e, D_EXPERT, d)
    b2r = b2.reshape(depth * n_e, 1, d)
    wr_p = jnp.pad(w_router, ((0, 0), (0, 0), (0, LANES - N_EXPERTS)))
    wr_hi = wr_p.astype(BF16)
    wr_lo = (wr_p - wr_hi.astype(F32)).astype(BF16)
    br_p = jnp.pad(b_router, ((0, 0), (0, LANES - N_EXPERTS)), constant_values=NEG_BIG)

    x_all = jnp.concatenate([x[0], ctx[0]], axis=0)
    for l in range(depth):
        last = l == depth - 1
        n_rows = s_len if last else nt
        modl = mod[l]
        gq_t = jnp.tile(q_gain[l], N_HEADS)[None, :]
        gk_t = jnp.tile(k_gain[l], KV_HEADS)[None, :]
        qa, kaT, va, qb, kbT, vb, ksq = _project(x_all, modl, w_in_b, l, gq_t, gk_t, bd,
                                                 cos_t, sa_t, sb_t, s_len, tm)
        o_a = _attn_a(sink[l], qa, kaT, va, n_rows, s_len)
        kmax = jnp.sqrt(jnp.max(ksq[:, :KV_HEADS], axis=0))
        o_b = _attn_b(kmax, qb, kbT, vb, n_rows, s_len)
        x1, h2_lin, top_i, top_w, cnt = _merge(
            o_a, o_b, x_all, modl, out_gain[l][None, :], w_out_b, l, ln1_g[l][None, :],
            ln1_b[l][None, :], wr_hi, wr_lo, br_p[l][None, :], n_rows, s_len, tm, alpha)
        pstart, te, tfirst, tvalid, txb, tzero = _tile_meta(cnt, n_rows * TOP_K, tm)
        eid = top_i[:, :TOP_K]
        of_expert = eid[:, :, None] == jnp.arange(N_EXPERTS, dtype=jnp.int32)[None, None, :]
        pos = (jnp.sum(jnp.where(of_expert, pstart[None, None, :], 0), axis=-1)
               + top_i[:, TOP_K:2 * TOP_K]).reshape(-1)
        xs_lin = _dispatch(pos, tzero, h2_lin, n_rows, d, tm)
        ys_lin = _experts(te, tfirst, tvalid, txb, xs_lin, w1p, b1p, w2r, b2r, l, d, tm)
        x_all = _combine(pos, ys_lin, top_w, x1, modl, ln2_g[l][None, :],
                         ln2_b[l][None, :], n_rows, s_len, tm, alpha)
    return x_all[None]
```

```python
import functools

import jax
import jax.numpy as jnp
from jax import lax
from jax.experimental import pallas as pl
from jax.experimental.pallas import tpu as pltpu

F32 = jnp.float32
BF16 = jnp.bfloat16

GRID_W = 64
HEAD_DIM = 64
N_HEADS = 16
KV_HEADS = 2
GROUP = N_HEADS // KV_HEADS
WINDOW = 128
BLOCK = 128
ROPE_THETA = 10000.0
ATTN_SCALE = HEAD_DIM ** -0.5
A_Q = N_HEADS * HEAD_DIM
A_KV = KV_HEADS * HEAD_DIM
N_EXPERTS = 32
TOP_K = 4
D_EXPERT = 512
SWIGLU_LIMIT = 7.0
SWIGLU_ALPHA = 1.702
LN_EPS = 1e-5
RMS_EPS = 1e-6
LOG2_E = 1.4426950408889634
LANES = 128
NEG_BIG = -1e30

VMEM_LIMIT = 56 * 1024 * 1024


def _cparams(sem):
    return pltpu.CompilerParams(dimension_semantics=sem, vmem_limit_bytes=VMEM_LIMIT)


def _ln(x):
    mu = jnp.mean(x, -1, keepdims=True)
    xc = x - mu
    var = jnp.mean(xc * xc, -1, keepdims=True)
    return xc * lax.rsqrt(var + LN_EPS)


def _mod_kernel(condT_ref, w_ref, b_ref, o_ref):
    ct = condT_ref[...]
    s = ct * jax.nn.sigmoid(ct)
    w = w_ref[...]
    r0 = jnp.sum(s[:, 0:1] * w, axis=0, keepdims=True)
    r1 = jnp.sum(s[:, 1:2] * w, axis=0, keepdims=True)
    o_ref[...] = jnp.concatenate([r0, r1], axis=0) + b_ref[...]


def _modulation(condT, w_ada, b_ada):
    depth, d, e = w_ada.shape
    tn = 512
    return pl.pallas_call(
        _mod_kernel,
        grid=(depth, e // tn),
        in_specs=[
            pl.BlockSpec((d, 2), lambda l, j: (0, 0)),
            pl.BlockSpec((None, d, tn), lambda l, j: (l, 0, j)),
            pl.BlockSpec((None, 1, tn), lambda l, j: (l, 0, j)),
        ],
        out_specs=pl.BlockSpec((None, 2, tn), lambda l, j: (l, 0, j)),
        out_shape=jax.ShapeDtypeStruct((depth, 2, e), F32),
        compiler_params=_cparams(("parallel", "parallel")),
        name="modulation",
    )(condT, w_ada, b_ada.reshape(depth, 1, e))


def _proj_kernel(x_ref, sh_ref, sc_ref, w_ref, gq_ref, gk_ref, bd_ref,
                 cos_ref, sa_ref, sb_ref,
                 qa_ref, kaT_ref, va_ref, qb_ref, kbT_ref, vb_ref, ksq_ref):
    x = x_ref[...]
    h = _ln(x) * (1.0 + sc_ref[...]) + sh_ref[...]
    p = jnp.dot(h.astype(BF16), w_ref[...], preferred_element_type=F32)
    cos, sa, sb = cos_ref[...], sa_ref[...], sb_ref[...]
    lane = lax.broadcasted_iota(jnp.int32, cos.shape, 1)

    def rope(t):
        return (t * cos + pltpu.roll(t, LANES - 16, 1) * sa
                + pltpu.roll(t, 16, 1) * sb)

    def head_rms(t, width):
        sq = t * t
        hi = sq.astype(BF16)
        lo = (sq - hi.astype(F32)).astype(BF16)
        bd = bd_ref[0:width, 0:width]
        ss = (jnp.dot(hi, bd, preferred_element_type=F32)
              + jnp.dot(lo, bd, preferred_element_type=F32))
        return lax.rsqrt(ss * (1.0 / HEAD_DIM) + RMS_EPS)

    def with_ones(t, g):
        src = t if g == 0 else pltpu.roll(t, HEAD_DIM, 1)
        return jnp.where(lane < HEAD_DIM, src,
                         jnp.where(lane == HEAD_DIM, 1.0, 0.0)).astype(BF16)

    def write_q(q_out, q, scale):
        for c in range(A_Q // LANES):
            r = rope(q[:, c * LANES:(c + 1) * LANES]) * scale
            q_out[2 * c] = r[:, :HEAD_DIM].astype(BF16)
            q_out[2 * c + 1] = r[:, HEAD_DIM:].astype(BF16)

    o = 0
    write_q(qa_ref, p[:, o:o + A_Q], ATTN_SCALE); o += A_Q
    kaT_ref[...] = rope(p[:, o:o + A_KV]).T.astype(BF16); o += A_KV
    va = p[:, o:o + A_KV]; o += A_KV
    va_ref[0] = with_ones(va, 0)
    va_ref[1] = with_ones(va, 1)
    qb = p[:, o:o + A_Q]; o += A_Q
    write_q(qb_ref, qb * head_rms(qb, A_Q) * gq_ref[...], ATTN_SCALE * LOG2_E)
    kb = p[:, o:o + A_KV]; o += A_KV
    kb = rope(kb * head_rms(kb, A_KV) * gk_ref[...])
    kbT_ref[...] = kb.T.astype(BF16)
    kq = kb.astype(BF16).astype(F32)
    ksq = kq * kq
    n0 = jnp.sum(jnp.where(lane < HEAD_DIM, ksq, 0.0), axis=1, keepdims=True)
    n1 = jnp.sum(jnp.where(lane >= HEAD_DIM, ksq, 0.0), axis=1, keepdims=True)
    ksq_ref[...] = jnp.where(lane == 0, n0, jnp.where(lane == 1, n1, 0.0))
    vb = p[:, o:o + A_KV]
    vb_ref[0] = with_ones(vb, 0)
    vb_ref[1] = with_ones(vb, 1)


def _project(x_all, modl, w_in_b, layer, gq_t, gk_t, bd, cos_t, sa_t, sb_t, n_lat_rows, tm):
    nt, d = x_all.shape
    n_lat_tiles = n_lat_rows // tm
    which = lambda i: jnp.where(i >= n_lat_tiles, 1, 0)
    modspec = lambda j: pl.BlockSpec((None, 1, d), lambda i: (which(i) * 6 + j, 0, 0))
    const2 = lambda a: pl.BlockSpec(a.shape, lambda i: (0, 0))
    layer3 = lambda a: pl.BlockSpec((None,) + a.shape[1:], lambda i: (layer, 0, 0))
    q_shape = jax.ShapeDtypeStruct((N_HEADS, nt, HEAD_DIM), BF16)
    kT_shape = jax.ShapeDtypeStruct((A_KV, nt), BF16)
    v_shape = jax.ShapeDtypeStruct((KV_HEADS, nt, LANES), BF16)
    q_spec = pl.BlockSpec((N_HEADS, tm, HEAD_DIM), lambda i: (0, i, 0))
    kT_spec = pl.BlockSpec((A_KV, tm), lambda i: (0, i))
    v_spec = pl.BlockSpec((KV_HEADS, tm, LANES), lambda i: (0, i, 0))
    tab_spec = pl.BlockSpec((tm, LANES), lambda i: (i, 0))
    return pl.pallas_call(
        _proj_kernel,
        grid=(nt // tm,),
        in_specs=[pl.BlockSpec((tm, d), lambda i: (i, 0)), modspec(0), modspec(1),
                  layer3(w_in_b), const2(gq_t), const2(gk_t), const2(bd),
                  tab_spec, tab_spec, tab_spec],
        out_specs=[q_spec, kT_spec, v_spec, q_spec, kT_spec, v_spec, tab_spec],
        out_shape=[q_shape, kT_shape, v_shape, q_shape, kT_shape, v_shape,
                   jax.ShapeDtypeStruct((nt, LANES), F32)],
        compiler_params=_cparams(("parallel",)),
        name="project",
    )(x_all, modl, modl, w_in_b, gq_t, gk_t, bd, cos_t, sa_t, sb_t)


FIXED_REF_LIMIT = 50.0


def _attn_b_kernel(kmax_ref, q_ref, kT_ref, v_ref, o_ref, m_scr, acc_scr,
                   *, n_lat_rows, tk, blocks_per_step):
    i = pl.program_id(0)
    tq = q_ref.shape[1]
    nt = kT_ref.shape[1]
    is_ctx = i * tq >= n_lat_rows
    acc_scr[...] = jnp.zeros(acc_scr.shape, F32)

    bound = jnp.zeros((tq, LANES), F32)
    for h in range(N_HEADS):
        qf = q_ref[h].astype(F32)
        qn = jnp.sqrt(jnp.sum(qf * qf, axis=1, keepdims=True))
        m_h = jnp.broadcast_to(qn * kmax_ref[h // GROUP], (tq, LANES))
        m_scr[h] = m_h
        bound = jnp.maximum(bound, m_h)
    fixed_ref_ok = jnp.max(bound) <= FIXED_REF_LIMIT

    def keys_fixed(off, width):
        for h in range(N_HEADS):
            g = h // GROUP
            kt = kT_ref[g * HEAD_DIM:(g + 1) * HEAD_DIM, pl.ds(off, width)]
            s = jnp.dot(q_ref[h], kt, preferred_element_type=F32)
            p = jnp.exp2(s - jnp.concatenate([m_scr[h]] * (width // LANES), axis=1))
            acc_scr[h] += jnp.dot(p.astype(BF16), v_ref[g, pl.ds(off, width), :],
                                  preferred_element_type=F32)

    def sweep(update, per_step):
        def step(j, carry):
            for u in range(per_step):
                update(pl.multiple_of(j * (tk * per_step) + u * tk, tk), tk)
            return carry

        lax.fori_loop(0, jnp.where(is_ctx, 0, n_lat_rows // (tk * per_step)), step, 0)
        update(n_lat_rows, nt - n_lat_rows)

    @pl.when(fixed_ref_ok)
    def _():
        per_all = blocks_per_step[2]
        if per_all:
            @pl.when(is_ctx)
            def _():
                keys_fixed(n_lat_rows, tk)

            @pl.when(jnp.logical_not(is_ctx))
            def _():
                def step(j, carry):
                    for u in range(per_all):
                        keys_fixed(pl.multiple_of(j * (tk * per_all) + u * tk, tk), tk)
                    return carry

                lax.fori_loop(0, nt // (tk * per_all), step, 0)
        else:
            sweep(keys_fixed, blocks_per_step[0])

    def keys(off, width):
        for h in range(N_HEADS):
            g = h // GROUP
            kt = kT_ref[g * HEAD_DIM:(g + 1) * HEAD_DIM, pl.ds(off, width)]
            s = jnp.dot(q_ref[h], kt, preferred_element_type=F32)
            m_prev = m_scr[h]
            m_new = jnp.maximum(m_prev, jnp.max(s, axis=1, keepdims=True))
            alpha = jnp.exp2(m_prev - m_new)
            p = jnp.exp2(s - jnp.concatenate([m_new] * (width // LANES), axis=1))
            pv = jnp.dot(p.astype(BF16), v_ref[g, pl.ds(off, width), :],
                         preferred_element_type=F32)
            acc_scr[h] = acc_scr[h] * alpha + pv
            m_scr[h] = m_new

    @pl.when(jnp.logical_not(fixed_ref_ok))
    def _():
        m_scr[...] = jnp.full(m_scr.shape, NEG_BIG, F32)
        sweep(keys, blocks_per_step[1])

    for h in range(N_HEADS):
        acc = acc_scr[h]
        o_ref[:, h * HEAD_DIM:(h + 1) * HEAD_DIM] = (
            acc[:, :HEAD_DIM] / acc[:, HEAD_DIM:HEAD_DIM + 1])


def _attn_b(kmax, q_hm, kT, v2, n_q_rows, n_lat_rows, tq=256, tk=256):
    nt = kT.shape[1]
    n_blocks = n_lat_rows // tk
    per_all = 0
    if nt - n_lat_rows == tk:
        per_all = max(b for b in range(1, 13) if (n_blocks + 1) % b == 0)
        per_all = per_all if per_all >= 3 else 0
    assert n_lat_rows % tk == 0 and n_blocks % 2 == 0
    blocks_per_step = (8 if n_blocks % 8 == 0 else 2, 2, per_all)
    return pl.pallas_call(
        functools.partial(_attn_b_kernel, n_lat_rows=n_lat_rows, tk=tk,
                          blocks_per_step=blocks_per_step),
        grid_spec=pltpu.PrefetchScalarGridSpec(
            num_scalar_prefetch=1,
            grid=(n_q_rows // tq,),
            in_specs=[pl.BlockSpec((N_HEADS, tq, HEAD_DIM), lambda i, s: (0, i, 0)),
                      pl.BlockSpec((A_KV, nt), lambda i, s: (0, 0)),
                      pl.BlockSpec((KV_HEADS, nt, LANES), lambda i, s: (0, 0, 0))],
            out_specs=pl.BlockSpec((tq, A_Q), lambda i, s: (i, 0)),
            scratch_shapes=[pltpu.VMEM((N_HEADS, tq, LANES), F32),
                            pltpu.VMEM((N_HEADS, tq, LANES), F32)],
        ),
        out_shape=jax.ShapeDtypeStruct((n_q_rows, A_Q), F32),
        compiler_params=_cparams(("parallel",)),
        name="attn_global",
    )(kmax, q_hm, kT, v2)


def _attn_a_kernel(sink_ref, q_ref, kT_ref, v_ref, o_ref, *, n_lat_rows):
    n = pl.program_id(0)
    n_lat_blocks = n_lat_rows // BLOCK
    nt = kT_ref.shape[1]
    is_lat = n < n_lat_blocks
    ii = lax.broadcasted_iota(jnp.int32, (BLOCK, BLOCK), 0)
    jj = lax.broadcasted_iota(jnp.int32, (BLOCK, BLOCK), 1)
    blocks = []
    for d in (-1, 0, 1):
        blk = n + d
        valid = is_lat & (blk >= 0) & (blk < n_lat_blocks)
        off = pl.multiple_of(jnp.clip(blk, 0, n_lat_blocks - 1) * BLOCK, BLOCK)
        rel_ok = (jj >= ii) if d == -1 else ((jj <= ii) if d == 1 else (jj >= 0))
        blocks.append((off, valid & rel_ok))

    for h in range(N_HEADS):
        g = h // GROUP
        q = q_ref[h]
        kg = kT_ref.at[g * HEAD_DIM:(g + 1) * HEAD_DIM, :]
        vg = v_ref.at[g]
        sink = sink_ref[h]
        s_list = []
        for off, mask in blocks:
            s = jnp.dot(q, kg[:, pl.ds(off, BLOCK)], preferred_element_type=F32)
            s_list.append(jnp.where(mask, s, NEG_BIG))
        s_ctx = jnp.dot(q, kg[:, n_lat_rows:nt], preferred_element_type=F32)
        col_max = s_list[0]
        for s in s_list[1:]:
            col_max = jnp.maximum(col_max, s)
        for c in range((nt - n_lat_rows) // BLOCK):
            col_max = jnp.maximum(col_max, s_ctx[:, c * BLOCK:(c + 1) * BLOCK])
        m = jnp.maximum(jnp.max(col_max, axis=1, keepdims=True), sink)
        acc = jnp.dot(jnp.exp(s_ctx - m).astype(BF16), vg[n_lat_rows:nt, :],
                      preferred_element_type=F32)
        for (off, _), s in zip(blocks, s_list):
            acc += jnp.dot(jnp.exp(s - m).astype(BF16), vg[pl.ds(off, BLOCK), :],
                           preferred_element_type=F32)
        denom = acc[:, HEAD_DIM:HEAD_DIM + 1] + jnp.exp(sink - m)
        o_ref[:, h * HEAD_DIM:(h + 1) * HEAD_DIM] = acc[:, :HEAD_DIM] / denom


def _attn_a(sink_l, q_hm, kT, v2, n_q_rows, n_lat_rows):
    nt = kT.shape[1]
    return pl.pallas_call(
        functools.partial(_attn_a_kernel, n_lat_rows=n_lat_rows),
        grid_spec=pltpu.PrefetchScalarGridSpec(
            num_scalar_prefetch=1,
            grid=(n_q_rows // BLOCK,),
            in_specs=[pl.BlockSpec((N_HEADS, BLOCK, HEAD_DIM), lambda i, s: (0, i, 0)),
                      pl.BlockSpec((A_KV, nt), lambda i, s: (0, 0)),
                      pl.BlockSpec((KV_HEADS, nt, LANES), lambda i, s: (0, 0, 0))],
            out_specs=pl.BlockSpec((BLOCK, A_Q), lambda i, s: (i, 0)),
        ),
        out_shape=jax.ShapeDtypeStruct((n_q_rows, A_Q), F32),
        compiler_params=_cparams(("parallel",)),
        name="attn_window",
    )(sink_l, q_hm, kT, v2)


def _w1_prep_kernel(w_ref, se_ref, so_ref, o_ref):
    wb = w_ref[...].astype(BF16)
    half = D_EXPERT // 2
    for b in range(2):
        blk = wb[:, 2 * half * b:2 * half * (b + 1)]
        o_ref[:, half * b:half * (b + 1)] = jnp.dot(
            blk, se_ref[...], preferred_element_type=F32).astype(BF16)
        o_ref[:, D_EXPERT + half * b:D_EXPERT + half * (b + 1)] = jnp.dot(
            blk, so_ref[...], preferred_element_type=F32).astype(BF16)


def _w1_prep(w1):
    depth, n_e, d, n2 = w1.shape
    j = jnp.arange(D_EXPERT)[:, None]
    m = jnp.arange(D_EXPERT // 2)[None, :]
    sel_even = (j == 2 * m).astype(BF16)
    sel_odd = (j == 2 * m + 1).astype(BF16)
    const2 = lambda a: pl.BlockSpec(a.shape, lambda i: (0, 0))
    out = pl.pallas_call(
        _w1_prep_kernel,
        grid=(depth * n_e,),
        in_specs=[pl.BlockSpec((None, d, n2), lambda i: (i, 0, 0)), const2(sel_even), const2(sel_odd)],
        out_specs=pl.BlockSpec((None, d, n2), lambda i: (i, 0, 0)),
        out_shape=jax.ShapeDtypeStruct((depth * n_e, d, n2), BF16),
        compiler_params=_cparams(("parallel",)),
        name="w1_prep",
    )(w1.reshape(depth * n_e, d, n2), sel_even, sel_odd)
    return out


def _merge_kernel(oa_ref, ob_ref, x_ref, og_ref, wout_ref, g1_ref, lg_ref, lb_ref,
                  sh2_ref, sc2_ref, wrh_ref, wrl_ref, br_ref,
                  x1_ref, h2_ref, ti_ref, tw_ref, cnt_ref, *, alpha):
    def rms(t, g):
        return t * lax.rsqrt(jnp.mean(t * t, -1, keepdims=True) + RMS_EPS) * g

    og = og_ref[...]
    o = jnp.concatenate([rms(oa_ref[...], og[:, :A_Q]), rms(ob_ref[...], og[:, A_Q:])], axis=1)
    mix = jnp.dot(o.astype(BF16), wout_ref[...], preferred_element_type=F32)
    x1 = _ln(alpha * x_ref[...] + g1_ref[...] * mix) * lg_ref[...] + lb_ref[...]
    x1_ref[...] = x1
    h2 = _ln(x1) * (1.0 + sc2_ref[...]) + sh2_ref[...]
    tm, d = h2.shape
    chunks = d // LANES
    for c in range(chunks):
        h2_ref[pl.ds(c, tm, stride=chunks), :] = h2[:, c * LANES:(c + 1) * LANES]

    h_hi = h2.astype(BF16)
    h_lo = (h2 - h_hi.astype(F32)).astype(BF16)
    w_hi = wrh_ref[...]
    logits = ((jnp.dot(h_hi, w_hi, preferred_element_type=F32)
               + jnp.dot(h_lo, w_hi, preferred_element_type=F32))
              + jnp.dot(h_hi, wrl_ref[...], preferred_element_type=F32)) + br_ref[...]
    lane = lax.broadcasted_iota(jnp.int32, logits.shape, 1)
    lane_f = lane.astype(F32)
    work = logits
    vals, ids = [], []
    for _ in range(TOP_K):
        mx = jnp.max(work, axis=1, keepdims=True)
        idx = jnp.min(jnp.where(work == mx, lane_f, float(LANES)), axis=1, keepdims=True)
        vals.append(mx)
        ids.append(idx)
        work = jnp.where(lane_f == idx, NEG_BIG, work)
    es = [jnp.exp(v - vals[0]) for v in vals]
    tot = es[0] + es[1] + es[2] + es[3]

    @pl.when(pl.program_id(0) == 0)
    def _():
        cnt_ref[...] = jnp.zeros(cnt_ref.shape, F32)

    onehot = [(lane_f == ids[k]).astype(F32) for k in range(TOP_K)]
    picked = (onehot[0] + onehot[1]) + (onehot[2] + onehot[3])
    earlier = (lax.broadcasted_iota(jnp.int32, (tm, tm), 1)
               < lax.broadcasted_iota(jnp.int32, (tm, tm), 0)).astype(BF16)
    before = jnp.dot(earlier, picked.astype(BF16), preferred_element_type=F32) + cnt_ref[...]
    ranks = [jnp.sum(onehot[k] * before, axis=1, keepdims=True) for k in range(TOP_K)]
    cnt_ref[...] += jnp.sum(picked, axis=0, keepdims=True)

    ti = jnp.zeros(logits.shape, jnp.int32)
    tw = jnp.zeros(logits.shape, F32)
    for k in range(TOP_K):
        ti = jnp.where(lane == k, ids[k].astype(jnp.int32), ti)
        ti = jnp.where(lane == TOP_K + k, ranks[k].astype(jnp.int32), ti)
        tw = jnp.where(lane == k, es[k] / tot, tw)
    ti_ref[...] = ti
    tw_ref[...] = tw


def _merge(o_a, o_b, x_all, modl, og, w_out_b, layer, lg, lb, wr_hi, wr_lo, br_p, n_rows,
           n_lat_rows, tm, alpha):
    d = x_all.shape[1]
    chunks = d // LANES
    n_lat_tiles = n_lat_rows // tm
    which = lambda i: jnp.where(i >= n_lat_tiles, 1, 0)
    modspec = lambda j: pl.BlockSpec((None, 1, d), lambda i: (which(i) * 6 + j, 0, 0))
    const2 = lambda a: pl.BlockSpec(a.shape, lambda i: (0, 0))
    layer3 = lambda a: pl.BlockSpec((None,) + a.shape[1:], lambda i: (layer, 0, 0))
    row = lambda w: pl.BlockSpec((tm, w), lambda i: (i, 0))
    return pl.pallas_call(
        functools.partial(_merge_kernel, alpha=alpha),
        grid=(n_rows // tm,),
        in_specs=[row(A_Q), row(A_Q), row(d), const2(og), layer3(w_out_b), modspec(2),
                  const2(lg), const2(lb), modspec(3), modspec(4), layer3(wr_hi), layer3(wr_lo),
                  const2(br_p)],
        out_specs=[row(d), pl.BlockSpec((tm * chunks, LANES), lambda i: (i, 0)),
                   row(LANES), row(LANES), pl.BlockSpec((1, LANES), lambda i: (0, 0))],
        out_shape=[jax.ShapeDtypeStruct((n_rows, d), F32),
                   jax.ShapeDtypeStruct((n_rows * chunks, LANES), F32),
                   jax.ShapeDtypeStruct((n_rows, LANES), jnp.int32),
                   jax.ShapeDtypeStruct((n_rows, LANES), F32),
                   jax.ShapeDtypeStruct((1, LANES), F32)],
        compiler_params=_cparams(("arbitrary",)),
        name="merge_route",
    )(o_a, o_b, x_all, og, w_out_b, modl, lg, lb, modl, modl, wr_hi, wr_lo, br_p)


def _tile_meta(cnt, n_flat, tm):
    i32 = jnp.int32
    counts = cnt[0, :N_EXPERTS].astype(i32)
    n_tiles = n_flat // tm + N_EXPERTS
    tiles_per = (counts + tm - 1) // tm
    tile_end = jnp.cumsum(tiles_per).astype(i32)
    tile_start = tile_end - tiles_per
    tid = jnp.arange(n_tiles, dtype=i32)
    used = tid < tile_end[-1]
    te = jnp.minimum(jnp.sum((tile_end[None, :] <= tid[:, None]).astype(i32), axis=1), N_EXPERTS - 1)
    onehot = te[:, None] == jnp.arange(N_EXPERTS, dtype=i32)[None, :]
    cnt_t = jnp.sum(jnp.where(onehot, counts[None, :], 0), axis=1)
    start_t = jnp.sum(jnp.where(onehot, tile_start[None, :], 0), axis=1)
    tvalid = jnp.where(used, jnp.clip(cnt_t - (tid - start_t) * tm, 0, tm), 0).astype(i32)
    tfirst = (used & (tid == start_t)).astype(i32)
    txb = jnp.where(used, tid, tile_end[-1] - 1).astype(i32)
    tzero = (tvalid < tm).astype(i32)
    return (tile_start * tm).astype(i32), te.astype(i32), tfirst, tvalid, txb, tzero


def _dispatch_kernel(pos_ref, tzero_ref, h2_ref, xs_hbm, zbuf, sem, zsem,
                     *, tb, tm, chunks, n_tiles):
    i = pl.program_id(0)

    @pl.when(i == 0)
    def _():
        zbuf[...] = jnp.zeros(zbuf.shape, F32)

        def fill(t):
            off = pl.multiple_of(t * (tm * chunks), tm * chunks)
            return pltpu.make_async_copy(zbuf, xs_hbm.at[pl.ds(off, tm * chunks), :], zsem)

        def start(t, c):
            @pl.when(tzero_ref[t] > 0)
            def _():
                fill(t).start()
            return c

        def wait(t, c):
            @pl.when(tzero_ref[t] > 0)
            def _():
                fill(t).wait()
            return c

        lax.fori_loop(0, n_tiles, start, 0)
        lax.fori_loop(0, n_tiles, wait, 0)

    def start_token(r, c):
        src = h2_ref.at[pl.ds(pl.multiple_of(r * chunks, chunks), chunks), :]
        base = (i * tb + r) * TOP_K
        for k in range(TOP_K):
            t = pl.multiple_of(pos_ref[base + k] * chunks, chunks)
            pltpu.make_async_copy(src, xs_hbm.at[pl.ds(t, chunks), :], sem).start(priority=k % 2)
        return c

    lax.fori_loop(0, tb, start_token, 0, unroll=2)
    for _ in range(TOP_K):
        pltpu.make_async_copy(h2_ref, xs_hbm.at[pl.ds(0, tb * chunks), :], sem).wait()


def _dispatch(pos, tzero, h2_lin, n_tok, d, tm, tb=256):
    chunks = d // LANES
    n_tiles = tzero.shape[0]
    return pl.pallas_call(
        functools.partial(_dispatch_kernel, tb=tb, tm=tm, chunks=chunks, n_tiles=n_tiles),
        grid_spec=pltpu.PrefetchScalarGridSpec(
            num_scalar_prefetch=2,
            grid=(n_tok // tb,),
            in_specs=[pl.BlockSpec((tb * chunks, LANES), lambda i, *_: (i, 0))],
            out_specs=pl.BlockSpec(memory_space=pl.ANY),
            scratch_shapes=[pltpu.VMEM((tm * chunks, LANES), F32),
                            pltpu.SemaphoreType.DMA, pltpu.SemaphoreType.DMA],
        ),
        out_shape=jax.ShapeDtypeStruct((n_tiles * tm * chunks, LANES), F32),
        compiler_params=_cparams(("arbitrary",)),
        name="dispatch",
    )(pos, tzero, h2_lin)


def _expert_kernel(te_ref, tfirst_ref, tvalid_ref, txb_ref,
                   x_ref, w1_ref, b1_ref, w2_ref, b2_ref, y_ref, w2b, *, tm, chunks):
    i = pl.program_id(0)

    @pl.when(tvalid_ref[i] > 0)
    def _():
        @pl.when(tfirst_ref[i] > 0)
        def _():
            w2b[...] = w2_ref[...].astype(BF16)

        x = jnp.concatenate([x_ref[pl.ds(c, tm, stride=chunks), :] for c in range(chunks)],
                            axis=1).astype(BF16)
        u = jnp.dot(x, w1_ref[...], preferred_element_type=F32) + b1_ref[...]
        glu = jnp.minimum(u[:, :D_EXPERT], SWIGLU_LIMIT)
        lin = jnp.clip(u[:, D_EXPERT:], -SWIGLU_LIMIT, SWIGLU_LIMIT)
        a = glu * jax.nn.sigmoid(SWIGLU_ALPHA * glu) * (lin + 1.0)
        y = jnp.dot(a.astype(BF16), w2b[...], preferred_element_type=F32) + b2_ref[...]
        for c in range(chunks):
            y_ref[pl.ds(c, tm, stride=chunks), :] = y[:, c * LANES:(c + 1) * LANES]

    @pl.when(tvalid_ref[i] == 0)
    def _():
        y_ref[...] = jnp.zeros(y_ref.shape, F32)


def _experts(te, tfirst, tvalid, txb, xs_lin, w1p, b1p, w2, b2, layer, d, tm):
    n_tiles = te.shape[0]
    chunks = d // LANES
    e0 = layer * N_EXPERTS
    return pl.pallas_call(
        functools.partial(_expert_kernel, tm=tm, chunks=chunks),
        grid_spec=pltpu.PrefetchScalarGridSpec(
            num_scalar_prefetch=4,
            grid=(n_tiles,),
            in_specs=[
                pl.BlockSpec((tm * chunks, LANES), lambda i, te, tf, tv, txb: (txb[i], 0)),
                pl.BlockSpec((None, d, 2 * D_EXPERT), lambda i, te, *_: (e0 + te[i], 0, 0)),
                pl.BlockSpec((None, 1, 2 * D_EXPERT), lambda i, te, *_: (e0 + te[i], 0, 0)),
                pl.BlockSpec((None, D_EXPERT, d), lambda i, te, *_: (e0 + te[i], 0, 0)),
                pl.BlockSpec((None, 1, d), lambda i, te, *_: (e0 + te[i], 0, 0)),
            ],
            out_specs=pl.BlockSpec((tm * chunks, LANES), lambda i, *_: (i, 0)),
            scratch_shapes=[pltpu.VMEM((D_EXPERT, d), BF16)],
        ),
        out_shape=jax.ShapeDtypeStruct((n_tiles * tm * chunks, LANES), F32),
        compiler_params=_cparams(("arbitrary",)),
        name="experts",
    )(te, tfirst, tvalid, txb, xs_lin, w1p, b1p, w2, b2)


def _combine_kernel(pos_ref, ys_hbm, tw_ref, x_ref, g2_ref, lg_ref, lb_ref,
                    o_ref, gbuf, sem, *, alpha, tm, chunks):
    i = pl.program_id(0)
    n = pl.num_programs(0)
    slot = i % 2

    def issue(step, slot_):
        def token(r, c):
            dst_rows = pl.ds(pl.multiple_of(r * chunks, chunks), chunks)
            base = (step * tm + r) * TOP_K
            for k in range(TOP_K):
                s = pl.multiple_of(pos_ref[base + k] * chunks, chunks)
                pltpu.make_async_copy(ys_hbm.at[pl.ds(s, chunks), :],
                                      gbuf.at[slot_, k, dst_rows, :],
                                      sem.at[slot_]).start(priority=k % 2)
            return c
        lax.fori_loop(0, tm, token, 0, unroll=2)

    @pl.when(i == 0)
    def _():
        issue(0, 0)

    @pl.when(i + 1 < n)
    def _():
        issue(i + 1, 1 - slot)

    for k in range(TOP_K):
        pltpu.make_async_copy(ys_hbm.at[pl.ds(0, tm * chunks), :], gbuf.at[slot, k],
                              sem.at[slot]).wait()

    tw = tw_ref[...]
    gate = [jnp.broadcast_to(tw[:, k:k + 1], (tm, LANES)) for k in range(TOP_K)]
    cols = []
    for c in range(chunks):
        sl = pl.ds(c, tm, stride=chunks)
        acc = gate[0] * gbuf[slot, 0, sl, :]
        for k in range(1, TOP_K):
            acc += gate[k] * gbuf[slot, k, sl, :]
        cols.append(acc)
    y = jnp.concatenate(cols, axis=1)
    o_ref[...] = _ln(alpha * x_ref[...] + g2_ref[...] * y) * lg_ref[...] + lb_ref[...]


def _combine(pos, ys_lin, top_w, x1, modl, lg, lb, n_rows, n_lat_rows, tm, alpha):
    d = x1.shape[1]
    chunks = d // LANES
    n_lat_tiles = n_lat_rows // tm
    which = lambda i: jnp.where(i >= n_lat_tiles, 1, 0)
    const2 = lambda a: pl.BlockSpec(a.shape, lambda i, *_: (0, 0))
    return pl.pallas_call(
        functools.partial(_combine_kernel, alpha=alpha, tm=tm, chunks=chunks),
        grid_spec=pltpu.PrefetchScalarGridSpec(
            num_scalar_prefetch=1,
            grid=(n_rows // tm,),
            in_specs=[pl.BlockSpec(memory_space=pl.ANY),
                      pl.BlockSpec((tm, LANES), lambda i, *_: (i, 0)),
                      pl.BlockSpec((tm, d), lambda i, *_: (i, 0)),
                      pl.BlockSpec((None, 1, d), lambda i, *_: (which(i) * 6 + 5, 0, 0)),
                      const2(lg), const2(lb)],
            out_specs=pl.BlockSpec((tm, d), lambda i, *_: (i, 0)),
            scratch_shapes=[pltpu.VMEM((2, TOP_K, tm * chunks, LANES), F32),
                            pltpu.SemaphoreType.DMA((2,))],
        ),
        out_shape=jax.ShapeDtypeStruct((n_rows, d), F32),
        compiler_params=_cparams(("arbitrary",)),
        name="combine",
    )(pos, ys_lin, top_w, x1, modl, lg, lb)


def _rope_tables(s_len, n_ctx):
    rows = s_len // GRID_W
    axis_dim = HEAD_DIM // 2
    row = jnp.repeat(jnp.arange(rows, dtype=F32), GRID_W)
    col = jnp.tile(jnp.arange(GRID_W, dtype=F32), rows)
    inv = ROPE_THETA ** (-jnp.arange(0, axis_dim, 2, dtype=F32) / axis_dim)
    ar = row[:, None] * inv[None, :]
    ac = col[:, None] * inv[None, :]
    ang = jnp.concatenate([ar, ar, ac, ac], -1)
    cos = jnp.tile(jnp.cos(ang), (1, 2))
    sin = jnp.tile(jnp.sin(ang), (1, 2))
    lo = (jnp.arange(LANES) % (axis_dim) < axis_dim // 2)[None, :]
    sa = jnp.where(lo, -sin, 0.0)
    sb = jnp.where(lo, 0.0, sin)
    pad = lambda t, v: jnp.concatenate([t, jnp.full((n_ctx, LANES), v, F32)], 0)
    return pad(cos, 1.0), pad(sa, 0.0), pad(sb, 0.0)


def kernel(x, c, ctx, c_ctx, w_ada, b_ada, w_in, q_gain, k_gain, sink, out_gain, w_out,
           ln1_g, ln1_b, w_router, b_router, w1, b1, w2, b2, ln2_g, ln2_b):
    bsz, s_len, d = x.shape
    n_ctx = ctx.shape[1]
    depth = w_ada.shape[0]
    assert bsz == 1 and d % LANES == 0
    nt = s_len + n_ctx
    alpha = (2.0 * depth) ** 0.25
    tm = 256

    condT = jnp.stack([c[0], c_ctx], axis=1)
    mod = _modulation(condT, w_ada, b_ada).reshape(depth, 12, 1, d)
    cos_t, sa_t, sb_t = _rope_tables(s_len, n_ctx)
    hid = jnp.arange(A_Q) // HEAD_DIM
    bd = (hid[:, None] == hid[None, :]).astype(BF16)
    w_in_b = w_in.astype(BF16)
    w_out_b = w_out.astype(BF16)
    n_e = w1.shape[1]
    w1p = _w1_prep(w1)
    b1p = jnp.concatenate([b1[..., 0::2], b1[..., 1::2]], axis=-1).reshape(depth * n_e, 1, -1)
    w2r = w2.reshape(depth * n_e, D_EXPERT, d)
    b2r = b2.reshape(depth * n_e, 1, d)
    wr_p = jnp.pad(w_router, ((0, 0), (0, 0), (0, LANES - N_EXPERTS)))
    wr_hi = wr_p.astype(BF16)
    wr_lo = (wr_p - wr_hi.astype(F32)).astype(BF16)
    br_p = jnp.pad(b_router, ((0, 0), (0, LANES - N_EXPERTS)), constant_values=NEG_BIG)

    x_all = jnp.concatenate([x[0], ctx[0]], axis=0)
    for l in range(depth):
        last = l == depth - 1
        n_rows = s_len if last else nt
        modl = mod[l]
        gq_t = jnp.tile(q_gain[l], N_HEADS)[None, :]
        gk_t = jnp.tile(k_gain[l], KV_HEADS)[None, :]
        qa, kaT, va, qb, kbT, vb, ksq = _project(x_all, modl, w_in_b, l, gq_t, gk_t, bd,
                                                 cos_t, sa_t, sb_t, s_len, tm)
        o_a = _attn_a(sink[l], qa, kaT, va, n_rows, s_len)
        kmax = jnp.sqrt(jnp.max(ksq[:, :KV_HEADS], axis=0))
        o_b = _attn_b(kmax, qb, kbT, vb, n_rows, s_len)
        x1, h2_lin, top_i, top_w, cnt = _merge(
            o_a, o_b, x_all, modl, out_gain[l][None, :], w_out_b, l, ln1_g[l][None, :],
            ln1_b[l][None, :], wr_hi, wr_lo, br_p[l][None, :], n_rows, s_len, tm, alpha)
        pstart, te, tfirst, tvalid, txb, tzero = _tile_meta(cnt, n_rows * TOP_K, tm)
        eid = top_i[:, :TOP_K]
        of_expert = eid[:, :, None] == jnp.arange(N_EXPERTS, dtype=jnp.int32)[None, None, :]
        pos = (jnp.sum(jnp.where(of_expert, pstart[None, None, :], 0), axis=-1)
               + top_i[:, TOP_K:2 * TOP_K]).reshape(-1)
        xs_lin = _dispatch(pos, tzero, h2_lin, n_rows, d, tm)
        ys_lin = _experts(te, tfirst, tvalid, txb, xs_lin, w1p, b1p, w2r, b2r, l, d, tm)
        x_all = _combine(pos, ys_lin, top_w, x1, modl, ln2_g[l][None, :],
                         ln2_b[l][None, :], n_rows, s_len, tm, alpha)
    return x_all[None]
```
